```python
import jax, jax.numpy as jnp
from jax import lax
import numpy as np

D_MODEL = 1024
BATCH = 8
SEQ = 2048
DEPTH = 4

N_MIXERS = 2
NSA_HEADS = 16
NSA_KV_GROUPS = 4
NSA_HEAD_DIM = D_MODEL // NSA_HEADS
NSA_HPG = NSA_HEADS // NSA_KV_GROUPS
NSA_KV_W = NSA_KV_GROUPS * NSA_HEAD_DIM
N_BRANCH = 3
CMP_LEN = 32
CMP_STRIDE = 16
CMP_HIDDEN = NSA_HEAD_DIM
SLC_LEN = 64
SLC_TOPN = 8
WINDOW = 512
Q_BLOCK = 128
FORCE_SCORE = 1.0e4
NEG_BIG = -1.0e30
NSA_SPLITS = [NSA_HEADS * NSA_HEAD_DIM + k * NSA_KV_W for k in range(0, 7)]
NSA_IN_W = NSA_HEADS * NSA_HEAD_DIM + 6 * NSA_KV_W + N_BRANCH * NSA_HEADS
SG_WIDTH = 2 * D_MODEL
SG_GROUPS = 16
SG_GROUP_W = SG_WIDTH // SG_GROUPS
SG_CHUNK = 128
FFN_HIDDEN = 2816
CONV_W = 3
EPS = 1e-6
N_NSA_LAYERS = (DEPTH + 1) // 2
N_SG_LAYERS = DEPTH // 2

kernel_name = "nsa_gmlp_interleaved_convffn_trunk"


def rmsnorm(x, g):
    xf = x.astype(jnp.float32)
    y = xf * lax.rsqrt(jnp.mean(xf * xf, axis=-1, keepdims=True) + EPS)
    return (y * g.astype(jnp.float32)).astype(x.dtype)


def layernorm(x, g, b):
    xf = x.astype(jnp.float32)
    mu = jnp.mean(xf, axis=-1, keepdims=True)
    var = jnp.mean(jnp.square(xf - mu), axis=-1, keepdims=True)
    y = (xf - mu) * lax.rsqrt(var + EPS)
    return (y * g.astype(jnp.float32) + b.astype(jnp.float32)).astype(x.dtype)


def masked_softmax(s, mask):
    s = jnp.where(mask, s.astype(jnp.float32), NEG_BIG)
    m = jnp.max(s, axis=-1, keepdims=True)
    e = jnp.where(mask, jnp.exp(s - m), 0.0)
    den = jnp.sum(e, axis=-1, keepdims=True)
    return e / jnp.maximum(den, 1e-30)


def compress_blocks(k, pe, w1, w2):
    T = k.shape[1]
    n_cmp = (T - CMP_LEN) // CMP_STRIDE + 1
    idx = np.arange(n_cmp)[:, None] * CMP_STRIDE + np.arange(CMP_LEN)[None, :]
    kb = k[:, idx] + pe[None, None, :, None, :]
    kb = jnp.transpose(kb, (0, 1, 3, 2, 4)).reshape(k.shape[0], n_cmp, NSA_KV_GROUPS, CMP_LEN * NSA_HEAD_DIM)
    h = jax.nn.gelu(jnp.einsum('bngf,fh->bngh', kb, w1))
    return jnp.einsum('bngh,hd->bngd', h, w2)


def cmp_to_slc_weights(n_cmp, n_slc):
    cs = np.arange(n_cmp)[:, None] * CMP_STRIDE
    ss = np.arange(n_slc)[None, :] * SLC_LEN
    ov = np.clip(np.minimum(cs + CMP_LEN, ss + SLC_LEN) - np.maximum(cs, ss), 0, None)
    return jnp.asarray(ov.astype(np.float32) / np.float32(CMP_LEN))


def nsa_mixer(x, w_in, cmp_pe, cmp_w1, cmp_w2, w_out):
    B, T, _ = x.shape
    G, HPG, DH = NSA_KV_GROUPS, NSA_HPG, NSA_HEAD_DIM
    scale = DH ** -0.5
    proj = x @ w_in
    q, kc, vc, ks, vs, kw, vw, gl = jnp.split(proj, NSA_SPLITS, axis=-1)
    q = q.reshape(B, T, G, HPG, DH)
    kc, vc, ks, vs, kw, vw = [a.reshape(B, T, G, DH) for a in (kc, vc, ks, vs, kw, vw)]
    gates = jax.nn.sigmoid(gl.astype(jnp.float32)).reshape(B, T, G, HPG, N_BRANCH)
    t = jnp.arange(T)

    kcmp = compress_blocks(kc, cmp_pe[0], cmp_w1[0], cmp_w2[0])
    vcmp = compress_blocks(vc, cmp_pe[1], cmp_w1[1], cmp_w2[1])
    n_cmp = kcmp.shape[1]
    cmp_end = jnp.arange(n_cmp) * CMP_STRIDE + CMP_LEN - 1
    cmp_mask = cmp_end[None, :] <= t[:, None]
    s_cmp = jnp.einsum('btghd,bngd->bghtn', q, kcmp) * scale
    p_cmp = masked_softmax(s_cmp, cmp_mask)
    o_cmp = jnp.einsum('bghtn,bngd->btghd', p_cmp.astype(vcmp.dtype), vcmp)

    n_slc = T // SLC_LEN
    imp = jnp.einsum('bghtn,nj->bgtj', p_cmp, cmp_to_slc_weights(n_cmp, n_slc))
    j = jnp.arange(n_slc)[None, :]
    blk = (t // SLC_LEN)[:, None]
    valid = j <= blk
    forced = (j == 0) | (j == blk) | (j == blk - 1)
    score = jnp.where(valid, imp + jnp.where(forced, FORCE_SCORE, 0.0), -FORCE_SCORE)
    n_sel = min(SLC_TOPN, n_slc)
    _, sel = lax.top_k(score, n_sel)

    ks_blocks = jnp.transpose(ks.reshape(B, n_slc, SLC_LEN, G, DH), (0, 3, 1, 2, 4))
    vs_blocks = jnp.transpose(vs.reshape(B, n_slc, SLC_LEN, G, DH), (0, 3, 1, 2, 4))
    kw_pad = jnp.pad(kw, ((0, 0), (WINDOW, 0), (0, 0), (0, 0)))
    vw_pad = jnp.pad(vw, ((0, 0), (WINDOW, 0), (0, 0), (0, 0)))
    gather = jax.vmap(jax.vmap(lambda blocks, ix: blocks[ix]))

    def block_fn(qb):
        s0 = qb * Q_BLOCK
        tq = s0 + jnp.arange(Q_BLOCK)
        qblk = lax.dynamic_slice_in_dim(q, s0, Q_BLOCK, axis=1)
        sel_b = lax.dynamic_slice_in_dim(sel, s0, Q_BLOCK, axis=2)
        kg = gather(ks_blocks, sel_b).reshape(B, G, Q_BLOCK, n_sel * SLC_LEN, DH)
        vg = gather(vs_blocks, sel_b).reshape(B, G, Q_BLOCK, n_sel * SLC_LEN, DH)
        kpos = (sel_b[..., None] * SLC_LEN + jnp.arange(SLC_LEN)).reshape(B, G, Q_BLOCK, n_sel * SLC_LEN)
        smask = (kpos <= tq[None, None, :, None])[:, :, None]
        s_s = jnp.einsum('bqghd,bgqkd->bghqk', qblk, kg) * scale
        p_s = masked_softmax(s_s, smask).astype(vg.dtype)
        o_s = jnp.einsum('bghqk,bgqkd->bqghd', p_s, vg)
        kwin = lax.dynamic_slice_in_dim(kw_pad, s0, WINDOW + Q_BLOCK, axis=1)
        vwin = lax.dynamic_slice_in_dim(vw_pad, s0, WINDOW + Q_BLOCK, axis=1)
        wpos = s0 - WINDOW + jnp.arange(WINDOW + Q_BLOCK)
        wmask = (wpos[None, :] <= tq[:, None]) & (wpos[None, :] > tq[:, None] - WINDOW) & (wpos[None, :] >= 0)
        s_w = jnp.einsum('bqghd,bkgd->bghqk', qblk, kwin) * scale
        p_w = masked_softmax(s_w, wmask).astype(vwin.dtype)
        o_w = jnp.einsum('bghqk,bkgd->bqghd', p_w, vwin)
        return o_s, o_w

    o_slc, o_win = lax.map(block_fn, jnp.arange(T // Q_BLOCK))
    o_slc = jnp.moveaxis(o_slc, 0, 1).reshape(B, T, G, HPG, DH)
    o_win = jnp.moveaxis(o_win, 0, 1).reshape(B, T, G, HPG, DH)

    o = (gates[..., 0:1] * o_cmp + gates[..., 1:2] * o_slc + gates[..., 2:3] * o_win).astype(x.dtype)
    return o.reshape(B, T, NSA_HEADS * DH) @ w_out


def sg_mixer(x, w_in, ln_g, ln_b, w_sp, b_sp, w_out):
    B, T, _ = x.shape
    z = jax.nn.gelu(x @ w_in)
    u, v = jnp.split(z, 2, axis=-1)
    v = layernorm(v, ln_g, ln_b)
    nc = T // SG_CHUNK
    v = v.reshape(B, nc, SG_CHUNK, SG_GROUPS, SG_GROUP_W)
    causal = jnp.tril(jnp.ones((SG_CHUNK, SG_CHUNK), dtype=bool))
    w = jnp.where(causal[None], w_sp, 0.0).astype(v.dtype)
    mixed = jnp.einsum('gts,bnsgf->bntgf', w, v) + b_sp.T[None, None, :, :, None]
    return (u * mixed.reshape(B, T, SG_WIDTH)) @ w_out


def conv_ffn(x, w_up, conv_w, conv_b, w_down):
    h = x @ w_up
    C = h.shape[-1]
    h = lax.conv_general_dilated(
        h, conv_w.reshape(CONV_W, 1, C).astype(h.dtype), window_strides=(1,),
        padding=[(CONV_W - 1, 0)], dimension_numbers=('NWC', 'WIO', 'NWC'),
        feature_group_count=C) + conv_b
    g, val = jnp.split(h, 2, axis=-1)
    return (jax.nn.silu(g) * val) @ w_down


def setup_inputs(seed: int = 0) -> dict:
    key = jax.random.key(seed)
    ks = jax.random.split(key, 18)
    nrm = jax.random.normal
    f32 = jnp.float32
    D, E, F, DH = D_MODEL, SG_WIDTH, FFN_HIDDEN, NSA_HEAD_DIM
    return {
        "x": nrm(ks[0], (BATCH, SEQ, D), f32),
        "norm_gains": 1.0 + 0.02 * nrm(ks[1], (DEPTH, 4, D), f32),
        "nsa_w_in": nrm(ks[2], (N_NSA_LAYERS, D, NSA_IN_W), f32) * D ** -0.5,
        "nsa_cmp_pe": 0.1 * nrm(ks[3], (N_NSA_LAYERS, 2, CMP_LEN, DH), f32),
        "nsa_cmp_w1": nrm(ks[4], (N_NSA_LAYERS, 2, CMP_LEN * DH, CMP_HIDDEN), f32) * (CMP_LEN * DH) ** -0.5,
        "nsa_cmp_w2": nrm(ks[5], (N_NSA_LAYERS, 2, CMP_HIDDEN, DH), f32) * CMP_HIDDEN ** -0.5,
        "nsa_w_out": nrm(ks[6], (N_NSA_LAYERS, NSA_HEADS * DH, D), f32) * (NSA_HEADS * DH) ** -0.5,
        "sg_w_in": nrm(ks[7], (N_SG_LAYERS, D, 2 * E), f32) * D ** -0.5,
        "sg_ln_g": 1.0 + 0.02 * nrm(ks[8], (N_SG_LAYERS, E), f32),
        "sg_ln_b": 0.02 * nrm(ks[9], (N_SG_LAYERS, E), f32),
        "sg_w_sp": nrm(ks[10], (N_SG_LAYERS, SG_GROUPS, SG_CHUNK, SG_CHUNK), f32) * SG_CHUNK ** -0.5,
        "sg_b_sp": 1.0 + 0.02 * nrm(ks[11], (N_SG_LAYERS, SG_GROUPS, SG_CHUNK), f32),
        "sg_w_out": nrm(ks[12], (N_SG_LAYERS, E, D), f32) * E ** -0.5,
        "ffn_w_up": nrm(ks[13], (DEPTH, D, 2 * F), f32) * D ** -0.5,
        "ffn_conv_w": nrm(ks[14], (DEPTH, CONV_W, 2 * F), f32) * CONV_W ** -0.5,
        "ffn_conv_b": 0.02 * nrm(ks[15], (DEPTH, 2 * F), f32),
        "ffn_w_down": nrm(ks[16], (DEPTH, F, D), f32) * F ** -0.5,
    }


def reference(x, norm_gains, nsa_w_in, nsa_cmp_pe, nsa_cmp_w1, nsa_cmp_w2, nsa_w_out,
              sg_w_in, sg_ln_g, sg_ln_b, sg_w_sp, sg_b_sp, sg_w_out,
              ffn_w_up, ffn_conv_w, ffn_conv_b, ffn_w_down):
    for i in range(DEPTH):
        g = norm_gains[i]
        slot = i // N_MIXERS
        h = rmsnorm(x, g[0])
        if i % N_MIXERS == 0:
            m = nsa_mixer(h, nsa_w_in[slot], nsa_cmp_pe[slot], nsa_cmp_w1[slot],
                          nsa_cmp_w2[slot], nsa_w_out[slot])
        else:
            m = sg_mixer(h, sg_w_in[slot], sg_ln_g[slot], sg_ln_b[slot],
                         sg_w_sp[slot], sg_b_sp[slot], sg_w_out[slot])
        x = x + rmsnorm(m, g[1])
        h = rmsnorm(x, g[2])
        x = x + rmsnorm(conv_ffn(h, ffn_w_up[i], ffn_conv_w[i], ffn_conv_b[i], ffn_w_down[i]), g[3])
    return x
```

```python
import functools

import numpy as np
import jax
import jax.numpy as jnp
from jax import lax
from jax.experimental import pallas as pl
from jax.experimental.pallas import tpu as pltpu

D_MODEL = 1024
SEQ = 2048
DEPTH = 4
N_MIXERS = 2
HEADS = 16
KV_GROUPS = 4
HEAD_DIM = 64
HPG = HEADS // KV_GROUPS
KV_W = KV_GROUPS * HEAD_DIM
N_BRANCH = 3
CMP_LEN = 32
CMP_STRIDE = 16
SLC_LEN = 64
SLC_TOPN = 8
WINDOW = 512
FORCE_SCORE = 1.0e4
NEG_BIG = -1.0e30
SG_WIDTH = 2 * D_MODEL
SG_GROUPS = 16
SG_GROUP_W = SG_WIDTH // SG_GROUPS
SG_CHUNK = 128
FFN_HIDDEN = 2816
CONV_W = 3
EPS = 1e-6

N_SLC = SEQ // SLC_LEN
N_CMP_PAD = SEQ // CMP_STRIDE
CHUNKS_PER_BLOCK = CMP_LEN // CMP_STRIDE
PROJ_W = HEADS * HEAD_DIM + 6 * KV_W
GATE_W = N_BRANCH * HEADS

LANES = 128
BF16_SUBLANES = 16
VMEM_LIMIT = 48 * 1024 * 1024

BF16 = jnp.bfloat16
F32 = jnp.float32


def _cparams(sem):
    return pltpu.CompilerParams(dimension_semantics=sem, vmem_limit_bytes=VMEM_LIMIT)


def _rms(x, g):
    return x * lax.rsqrt(jnp.mean(x * x, axis=-1, keepdims=True) + EPS) * g


def _dot(a, b):
    return jnp.dot(a, b, preferred_element_type=F32)


def _dot_nt(a, b):
    return lax.dot_general(a, b, (((1,), (1,)), ((), ())), preferred_element_type=F32)


def _norm_matmul_kernel(x_ref, g_ref, w_ref, o_ref, xn_ref, *, act):
    @pl.when(pl.program_id(1) == 0)
    def _():
        xn_ref[...] = _rms(x_ref[...], g_ref[...]).astype(BF16)

    acc = _dot(xn_ref[...], w_ref[...])
    if act:
        acc = jax.nn.gelu(acc)
    o_ref[...] = acc.astype(o_ref.dtype)


def norm_matmul(x, g, w, *, act, tm, tn):
    m, k = x.shape
    n = w.shape[1]
    return pl.pallas_call(
        functools.partial(_norm_matmul_kernel, act=act),
        grid=(m // tm, n // tn),
        in_specs=[
            pl.BlockSpec((tm, k), lambda i, j: (i, 0)),
            pl.BlockSpec((1, k), lambda i, j: (0, 0)),
            pl.BlockSpec((k, tn), lambda i, j: (0, j)),
        ],
        out_specs=pl.BlockSpec((tm, tn), lambda i, j: (i, j)),
        out_shape=jax.ShapeDtypeStruct((m, n), F32),
        scratch_shapes=[pltpu.VMEM((tm, k), BF16)],
        compiler_params=_cparams(("parallel", "arbitrary")),
        name="norm_matmul_gelu" if act else "norm_matmul",
    )(x, g, w)


def _compress_kernel(x0_ref, x1_ref, x2_ref, x3_ref, pe_ref, wa_ref, wb_ref, w2_ref, k_ref, v_ref,
                     ca_ref, cb_ref):
    nck = N_CMP_PAD
    for l in range(CMP_STRIDE):
        for g, x_ref in enumerate((x0_ref, x1_ref, x2_ref, x3_ref)):
            xg = x_ref[pl.ds(l, nck, stride=CMP_STRIDE), :]
            ca_ref[nck * g:nck * (g + 1), LANES * l:LANES * (l + 1)] = (
                xg + pe_ref[l:l + 1, :]).astype(BF16)
            cb_ref[nck * g:nck * (g + 1), LANES * l:LANES * (l + 1)] = (
                xg + pe_ref[CMP_STRIDE + l:CMP_STRIDE + l + 1, :]).astype(BF16)
    a = _dot(ca_ref[...], wa_ref[...])
    bm = _dot(cb_ref[...], wb_ref[...])
    for g in range(KV_GROUPS):
        ag = a[nck * g:nck * (g + 1)]
        bg = bm[nck * g:nck * (g + 1)]
        pre = ag + pltpu.roll(bg, nck - 1, 0)
        h = jax.nn.gelu(pre).astype(BF16)
        out = _dot(h, w2_ref[...])
        k_ref[nck * g:nck * (g + 1), :] = out[:, :HEAD_DIM]
        v_ref[nck * g:nck * (g + 1), :] = out[:, HEAD_DIM:]


def nsa_compress(proj3, pe_cat, wa, wb, w2bd):
    b = proj3.shape[0]
    rows = KV_GROUPS * N_CMP_PAD
    feat = CMP_STRIDE * LANES
    base = HEADS * HEAD_DIM // LANES
    x_specs = [pl.BlockSpec((None, SEQ, LANES), functools.partial(lambda i, g: (i, 0, base + g), g=g))
               for g in range(KV_GROUPS)]
    return pl.pallas_call(
        _compress_kernel,
        grid=(b,),
        in_specs=x_specs + [
            pl.BlockSpec((CMP_LEN, LANES), lambda i: (0, 0)),
            pl.BlockSpec((feat, LANES), lambda i: (0, 0)),
            pl.BlockSpec((feat, LANES), lambda i: (0, 0)),
            pl.BlockSpec((LANES, LANES), lambda i: (0, 0)),
        ],
        out_specs=[
            pl.BlockSpec((None, rows, HEAD_DIM), lambda i: (i, 0, 0)),
            pl.BlockSpec((None, rows, HEAD_DIM), lambda i: (i, 0, 0)),
        ],
        out_shape=[jax.ShapeDtypeStruct((b, rows, HEAD_DIM), F32)] * 2,
        scratch_shapes=[pltpu.VMEM((rows, feat), BF16), pltpu.VMEM((rows, feat), BF16)],
        compiler_params=_cparams(("parallel",)),
        name="nsa_compress",
    )(proj3, proj3, proj3, proj3, pe_cat, wa, wb, w2bd)


def _stack_heads(q):
    return jnp.concatenate([q[:, HEAD_DIM * h:HEAD_DIM * (h + 1)] for h in range(HPG)], axis=0)


def _unstack_heads(o, tq):
    return jnp.concatenate([o[tq * h:tq * (h + 1)] for h in range(HPG)], axis=1)


def _cmp_select_kernel(q_ref, k_ref, v_ref, wt_ref, o_ref, sel_ref, *, tq):
    t0 = pl.program_id(2) * tq
    scale = HEAD_DIM ** -0.5
    q4 = _stack_heads(q_ref[...]).astype(BF16)
    kc = k_ref[...].astype(BF16)
    vc = v_ref[...].astype(BF16)
    s = _dot_nt(q4, kc) * scale
    shape = s.shape
    t = t0 + (lax.broadcasted_iota(jnp.int32, shape, 0) & (tq - 1))
    n = lax.broadcasted_iota(jnp.int32, shape, 1)
    mask = (n * CMP_STRIDE + (CMP_LEN - 1)) <= t
    s = jnp.where(mask, s, NEG_BIG)
    m = jnp.max(s, axis=-1, keepdims=True)
    e = jnp.where(mask, jnp.exp(s - m), 0.0)
    den = jnp.sum(e, axis=-1, keepdims=True)
    p = (e / jnp.maximum(den, 1e-30)).astype(BF16)
    o_ref[...] = _unstack_heads(_dot(p, vc), tq)

    imp4 = _dot_nt(wt_ref[...], p)
    imp = imp4[:, 0:tq]
    for h in range(1, HPG):
        imp = imp + imp4[:, tq * h:tq * (h + 1)]
    j = lax.broadcasted_iota(jnp.int32, imp.shape, 0)
    blk = (t0 + lax.broadcasted_iota(jnp.int32, imp.shape, 1)) >> 6
    valid = j <= blk
    forced = (j == 0) | (j == blk) | (j == blk - 1)
    score = jnp.where(valid, imp + jnp.where(forced, FORCE_SCORE, 0.0), -FORCE_SCORE)
    rank = jnp.zeros(imp.shape, F32)
    for i in range(N_SLC):
        ri = score[i:i + 1, :]
        beats = jnp.where(ri > score, 1.0, jnp.where(ri == score, jnp.where(j > i, 1.0, 0.0), 0.0))
        rank = rank + beats
    sel_t = jnp.where(rank < float(SLC_TOPN), 1.0, 0.0)
    sel_t = jnp.concatenate([sel_t, jnp.zeros((LANES - N_SLC, tq), F32)], axis=0)
    sel_ref[...] = sel_t.T.astype(BF16)


def nsa_cmp_select(proj3, kcmp, vcmp, wcs_t, *, tq):
    b = proj3.shape[0]
    qw = HPG * HEAD_DIM
    return pl.pallas_call(
        functools.partial(_cmp_select_kernel, tq=tq),
        grid=(b, KV_GROUPS, SEQ // tq),
        in_specs=[
            pl.BlockSpec((None, tq, qw), lambda b_, g, i: (b_, i, g)),
            pl.BlockSpec((None, N_CMP_PAD, HEAD_DIM), lambda b_, g, i: (b_, g, 0)),
            pl.BlockSpec((None, N_CMP_PAD, HEAD_DIM), lambda b_, g, i: (b_, g, 0)),
            pl.BlockSpec((N_SLC, N_CMP_PAD), lambda b_, g, i: (0, 0)),
        ],
        out_specs=[
            pl.BlockSpec((None, tq, qw), lambda b_, g, i: (b_, i, g)),
            pl.BlockSpec((None, None, tq, LANES), lambda b_, g, i: (b_, g, i, 0)),
        ],
        out_shape=[
            jax.ShapeDtypeStruct((b, SEQ, HEADS * HEAD_DIM), F32),
            jax.ShapeDtypeStruct((b, KV_GROUPS, SEQ, LANES), BF16),
        ],
        compiler_params=_cparams(("parallel", "parallel", "parallel")),
        name="nsa_cmp_select",
    )(proj3, kcmp, vcmp, wcs_t)


def _flash_step(q4, kv, mask, carry, scale):
    m, l, acc = carry
    s = _dot_nt(q4, kv) * scale
    s = jnp.where(mask, s, NEG_BIG)
    m_new = jnp.maximum(m, jnp.max(s, axis=-1, keepdims=True))
    alpha = jnp.exp(m - m_new)
    e = jnp.where(mask, jnp.exp(s - m_new), 0.0)
    l = alpha * l + jnp.sum(e, axis=-1, keepdims=True)
    acc = alpha * acc + _dot(e.astype(BF16), kv)
    return m_new, l, acc


def _slc_win_kernel(q_ref, sel_ref, kvs_ref, kvw_ref, e_ref, os_ref, ow_ref, *, tq, tk):
    qi = pl.program_id(2)
    t0 = qi * tq
    scale = HEAD_DIM ** -0.5
    rows = HPG * tq
    q = q_ref[...]
    zpad = jnp.zeros((tq, HEAD_DIM), F32)
    q4 = jnp.concatenate(
        [jnp.concatenate([q[:, HEAD_DIM * h:HEAD_DIM * (h + 1)], zpad], axis=1) for h in range(HPG)],
        axis=0).astype(BF16)
    sel = sel_ref[...]
    trow = t0 + (lax.broadcasted_iota(jnp.int32, (rows, tk), 0) & (tq - 1))
    lane = lax.broadcasted_iota(jnp.int32, (rows, tk), 1)
    init = (jnp.full((rows, 1), NEG_BIG, F32), jnp.zeros((rows, 1), F32), jnp.zeros((rows, LANES), F32))
    n_kt = (t0 + tq) // tk

    def slc_body(kt, carry):
        k0 = pl.multiple_of(kt * tk, tk)
        kv = kvs_ref[pl.ds(k0, tk), :].astype(BF16)
        selm = _dot(sel, e_ref[kt])
        selm4 = jnp.concatenate([selm] * HPG, axis=0)
        d = trow - (k0 + lane)
        mask = jnp.where(d >= 0, selm4, 0.0) > 0.5
        return _flash_step(q4, kv, mask, carry, scale)

    _, l_s, acc_s = lax.fori_loop(0, n_kt, slc_body, init)
    os_ref[...] = _unstack_heads((acc_s / l_s)[:, HEAD_DIM:], tq)

    def win_body(kt, carry):
        k0 = pl.multiple_of(kt * tk, tk)
        kv = kvw_ref[pl.ds(k0, tk), :].astype(BF16)
        d = trow - (k0 + lane)
        mask = jnp.where(d >= 0, d, WINDOW) < WINDOW
        return _flash_step(q4, kv, mask, carry, scale)

    kt_lo = jnp.maximum(t0 - WINDOW, 0) // tk
    _, l_w, acc_w = lax.fori_loop(kt_lo, n_kt, win_body, init)
    ow_ref[...] = _unstack_heads((acc_w / l_w)[:, HEAD_DIM:], tq)


def nsa_slc_win(proj3, sel, expand, *, tq, tk):
    b = proj3.shape[0]
    qw = HPG * HEAD_DIM
    q_blocks = HEADS * HEAD_DIM // LANES
    kvs_base = q_blocks + KV_GROUPS
    kvw_base = kvs_base + KV_GROUPS
    o_spec = pl.BlockSpec((None, tq, qw), lambda b_, g, i: (b_, i, g))
    o_shape = jax.ShapeDtypeStruct((b, SEQ, HEADS * HEAD_DIM), F32)
    return pl.pallas_call(
        functools.partial(_slc_win_kernel, tq=tq, tk=tk),
        grid=(b, KV_GROUPS, SEQ // tq),
        in_specs=[
            pl.BlockSpec((None, tq, qw), lambda b_, g, i: (b_, i, g)),
            pl.BlockSpec((None, None, tq, LANES), lambda b_, g, i: (b_, g, i, 0)),
            pl.BlockSpec((None, SEQ, LANES), lambda b_, g, i: (b_, 0, kvs_base + g)),
            pl.BlockSpec((None, SEQ, LANES), lambda b_, g, i: (b_, 0, kvw_base + g)),
            pl.BlockSpec((SEQ // tk, LANES, tk), lambda b_, g, i: (0, 0, 0)),
        ],
        out_specs=[o_spec, o_spec],
        out_shape=[o_shape, o_shape],
        compiler_params=_cparams(("parallel", "parallel", "parallel")),
        name="nsa_slc_win",
    )(proj3, sel, proj3, proj3, expand)


def _nsa_out_kernel(x_ref, g0_ref, wg_ref, oc_ref, os_ref, ow_ref, wo_ref, g1_ref, out_ref):
    x = x_ref[...]
    gl = _dot(_rms(x, g0_ref[...]).astype(BF16), wg_ref[...])
    sg = jax.nn.sigmoid(gl)
    lane = lax.broadcasted_iota(jnp.int32, sg.shape, 1)
    lower = lane < HEAD_DIM
    pieces = []
    for kb in range(HEADS * HEAD_DIM // LANES):
        h0, h1 = 2 * kb, 2 * kb + 1
        acc = None
        for r, o_ref in enumerate((oc_ref, os_ref, ow_ref)):
            c0, c1 = N_BRANCH * h0 + r, N_BRANCH * h1 + r
            gexp = jnp.where(lower, sg[:, c0:c0 + 1], sg[:, c1:c1 + 1])
            term = gexp * o_ref[:, LANES * kb:LANES * (kb + 1)]
            acc = term if acc is None else acc + term
        pieces.append(acc.astype(BF16))
    o = jnp.concatenate(pieces, axis=1)
    m = _dot(o, wo_ref[...])
    out_ref[...] = x + _rms(m, g1_ref[...])


def nsa_out(x, g0, w_gate, o_cmp, o_slc, o_win, w_out, g1, *, tm):
    m, d = x.shape
    row = pl.BlockSpec((tm, d), lambda i: (i, 0))
    vec = pl.BlockSpec((1, d), lambda i: (0, 0))
    return pl.pallas_call(
        _nsa_out_kernel,
        grid=(m // tm,),
        in_specs=[row, vec, pl.BlockSpec((d, LANES), lambda i: (0, 0)), row, row, row,
                  pl.BlockSpec((d, d), lambda i: (0, 0)), vec],
        out_specs=row,
        out_shape=jax.ShapeDtypeStruct((m, d), F32),
        compiler_params=_cparams(("parallel",)),
        name="nsa_out",
    )(x, g0, w_gate, o_cmp, o_slc, o_win, w_out, g1)


def _sg_out_kernel(x_ref, u_ref, v_ref, lng_ref, lnb_ref, wsp_ref, bt_ref, wo_ref, g1_ref,
                   out_ref, wm_ref, gated_ref, *, tm):
    @pl.when(pl.program_id(0) == 0)
    def _():
        t = lax.broadcasted_iota(jnp.int32, (SG_CHUNK, SG_CHUNK), 0)
        s = lax.broadcasted_iota(jnp.int32, (SG_CHUNK, SG_CHUNK), 1)
        for g in range(SG_GROUPS):
            wm_ref[g] = jnp.where(s <= t, wsp_ref[g], 0.0).astype(BF16)

    v = v_ref[...]
    mu = jnp.mean(v, axis=-1, keepdims=True)
    vc = v - mu
    var = jnp.mean(vc * vc, axis=-1, keepdims=True)
    vn = (vc * lax.rsqrt(var + EPS) * lng_ref[...] + lnb_ref[...]).astype(BF16)
    for c in range(tm // SG_CHUNK):
        r0, r1 = SG_CHUNK * c, SG_CHUNK * (c + 1)
        for g in range(SG_GROUPS):
            c0, c1 = SG_GROUP_W * g, SG_GROUP_W * (g + 1)
            mixed = _dot(wm_ref[g], vn[r0:r1, c0:c1]) + bt_ref[:, g:g + 1]
            gated_ref[r0:r1, c0:c1] = (u_ref[r0:r1, c0:c1] * mixed).astype(BF16)
    m = _dot(gated_ref[...], wo_ref[...])
    out_ref[...] = x_ref[...] + _rms(m, g1_ref[...])


def sg_out(x, z, ln_g, ln_b, w_sp, b_sp_t, w_out, g1, *, tm):
    m, d = x.shape
    e = SG_WIDTH
    row = pl.BlockSpec((tm, d), lambda i: (i, 0))
    return pl.pallas_call(
        functools.partial(_sg_out_kernel, tm=tm),
        grid=(m // tm,),
        in_specs=[
            row,
            pl.BlockSpec((tm, e), lambda i: (i, 0)),
            pl.BlockSpec((tm, e), lambda i: (i, 1)),
            pl.BlockSpec((1, e), lambda i: (0, 0)),
            pl.BlockSpec((1, e), lambda i: (0, 0)),
            pl.BlockSpec((SG_GROUPS, SG_CHUNK, SG_CHUNK), lambda i: (0, 0, 0)),
            pl.BlockSpec((SG_CHUNK, SG_GROUPS), lambda i: (0, 0)),
            pl.BlockSpec((e, d), lambda i: (0, 0)),
            pl.BlockSpec((1, d), lambda i: (0, 0)),
        ],
        out_specs=row,
        out_shape=jax.ShapeDtypeStruct((m, d), F32),
        scratch_shapes=[pltpu.VMEM((SG_GROUPS, SG_CHUNK, SG_CHUNK), BF16), pltpu.VMEM((tm, e), BF16)],
        compiler_params=_cparams(("arbitrary",)),
        name="sg_out",
    )(x, z, z, ln_g, ln_b, w_sp, b_sp_t, w_out, g1)


HALO = BF16_SUBLANES


def _ffn_kernel(x_ref, xh_ref, g2_ref, wg_ref, wv_ref, cwg_ref, cwv_ref, cbg_ref, cbv_ref,
                wd_ref, g3_ref, out_ref, xn_ref, acc_ref, hg_ref, hv_ref, *, tm, tiles_per_seq):
    i = pl.program_id(0)
    j = pl.program_id(1)

    @pl.when(j == 0)
    def _():
        xn_ref[HALO:, :] = _rms(x_ref[...], g2_ref[...]).astype(BF16)
        hn = _rms(xh_ref[...], g2_ref[...])
        seq_start = (i % tiles_per_seq) == 0
        xn_ref[:HALO, :] = jnp.where(seq_start, 0.0, hn).astype(BF16)
        acc_ref[...] = jnp.zeros_like(acc_ref)

    xa = xn_ref[...]
    hg_ref[...] = _dot(xa, wg_ref[...])
    hv_ref[...] = _dot(xa, wv_ref[...])

    def conv(h_ref, cw_ref, cb_ref):
        out = cb_ref[...] + cw_ref[CONV_W - 1:CONV_W, :] * h_ref[HALO:, :]
        for k in range(CONV_W - 1):
            off = HALO - (CONV_W - 1 - k)
            out = out + cw_ref[k:k + 1, :] * h_ref[off:off + tm, :]
        return out

    cg = conv(hg_ref, cwg_ref, cbg_ref)
    cv = conv(hv_ref, cwv_ref, cbv_ref)
    act = (jax.nn.silu(cg) * cv).astype(BF16)
    acc_ref[...] += _dot(act, wd_ref[...])

    @pl.when(j == pl.num_programs(1) - 1)
    def _():
        out_ref[...] = x_ref[...] + _rms(acc_ref[...], g3_ref[...])


def conv_ffn(x, g2, w_up, conv_w, conv_b, w_down, g3, *, tm, tf):
    m, d = x.shape
    f = FFN_HIDDEN
    nf = f // tf
    tiles_per_seq = SEQ // tm
    halo_blocks = tm // HALO
    vec = pl.BlockSpec((1, d), lambda i, j: (0, 0))
    return pl.pallas_call(
        functools.partial(_ffn_kernel, tm=tm, tiles_per_seq=tiles_per_seq),
        grid=(m // tm, nf),
        in_specs=[
            pl.BlockSpec((tm, d), lambda i, j: (i, 0)),
            pl.BlockSpec((HALO, d), lambda i, j: (jnp.maximum(i * halo_blocks - 1, 0), 0)),
            vec,
            pl.BlockSpec((d, tf), lambda i, j: (0, j)),
            pl.BlockSpec((d, tf), lambda i, j: (0, nf + j)),
            pl.BlockSpec((CONV_W, tf), lambda i, j: (0, j)),
            pl.BlockSpec((CONV_W, tf), lambda i, j: (0, nf + j)),
            pl.BlockSpec((1, tf), lambda i, j: (0, j)),
            pl.BlockSpec((1, tf), lambda i, j: (0, nf + j)),
            pl.BlockSpec((tf, d), lambda i, j: (j, 0)),
            vec,
        ],
        out_specs=pl.BlockSpec((tm, d), lambda i, j: (i, 0)),
        out_shape=jax.ShapeDtypeStruct((m, d), F32),
        scratch_shapes=[
            pltpu.VMEM((HALO + tm, d), BF16),
            pltpu.VMEM((tm, d), F32),
            pltpu.VMEM((HALO + tm, tf), F32),
            pltpu.VMEM((HALO + tm, tf), F32),
        ],
        compiler_params=_cparams(("parallel", "arbitrary")),
        name="conv_ffn",
    )(x, x, g2, w_up, w_up, conv_w, conv_w, conv_b, conv_b, w_down, g3)


def _nsa_column_order():
    q_w = HEADS * HEAD_DIM
    order = list(range(q_w))
    for pair in range(3):
        k_base = q_w + (2 * pair) * KV_W
        v_base = q_w + (2 * pair + 1) * KV_W
        for g in range(KV_GROUPS):
            order += list(range(k_base + HEAD_DIM * g, k_base + HEAD_DIM * (g + 1)))
            order += list(range(v_base + HEAD_DIM * g, v_base + HEAD_DIM * (g + 1)))
    return np.asarray(order, dtype=np.int32)


def _slc_expand(tk):
    e = np.zeros((LANES, SEQ), dtype=np.float32)
    key = np.arange(SEQ)
    e[key // SLC_LEN, key] = 1.0
    e = e.reshape(LANES, SEQ // tk, tk).transpose(1, 0, 2)
    return jnp.asarray(e, dtype=BF16)


def _cmp_to_slc_t():
    n_cmp = (SEQ - CMP_LEN) // CMP_STRIDE + 1
    cs = np.arange(n_cmp)[:, None] * CMP_STRIDE
    ss = np.arange(N_SLC)[None, :] * SLC_LEN
    ov = np.clip(np.minimum(cs + CMP_LEN, ss + SLC_LEN) - np.maximum(cs, ss), 0, None)
    w = ov.astype(np.float32) / np.float32(CMP_LEN)
    wt = np.zeros((N_SLC, N_CMP_PAD), dtype=np.float32)
    wt[:, :n_cmp] = w.T
    return jnp.asarray(wt, dtype=BF16)


def _block_diag2(a, b):
    za = jnp.zeros((a.shape[0], b.shape[1]), a.dtype)
    zb = jnp.zeros((b.shape[0], a.shape[1]), a.dtype)
    return jnp.concatenate([jnp.concatenate([a, za], axis=1), jnp.concatenate([zb, b], axis=1)], axis=0)


def _nsa_layer(x2, batch, g, w_in, cmp_pe, cmp_w1, cmp_w2, w_out):
    order = _nsa_column_order()
    w_main = jnp.take(w_in, order, axis=1).astype(BF16)
    w_gate = jnp.pad(w_in[:, PROJ_W:], ((0, 0), (0, LANES - GATE_W))).astype(BF16)
    w1 = cmp_w1.reshape(2, CMP_LEN, HEAD_DIM, HEAD_DIM)
    w1bd = jnp.stack([_block_diag2(w1[0, l], w1[1, l]) for l in range(CMP_LEN)])
    wa = w1bd[:CMP_STRIDE].reshape(CMP_STRIDE * LANES, LANES).astype(BF16)
    wb = w1bd[CMP_STRIDE:].reshape(CMP_STRIDE * LANES, LANES).astype(BF16)
    w2bd = _block_diag2(cmp_w2[0], cmp_w2[1]).astype(BF16)
    pe_cat = jnp.concatenate([cmp_pe[0], cmp_pe[1]], axis=1)

    proj = norm_matmul(x2, g[0][None], w_main, act=False, tm=512, tn=512)
    proj3 = proj.reshape(batch, SEQ, PROJ_W)
    kcmp, vcmp = nsa_compress(proj3, pe_cat, wa, wb, w2bd)
    o_cmp, sel = nsa_cmp_select(proj3, kcmp, vcmp, _cmp_to_slc_t(), tq=256)
    o_slc, o_win = nsa_slc_win(proj3, sel, _slc_expand(128), tq=128, tk=128)
    md = HEADS * HEAD_DIM
    return nsa_out(x2, g[0][None], w_gate, o_cmp.reshape(-1, md), o_slc.reshape(-1, md),
                   o_win.reshape(-1, md), w_out.astype(BF16), g[1][None], tm=256)


def _sg_layer(x2, g, w_in, ln_g, ln_b, w_sp, b_sp, w_out):
    z = norm_matmul(x2, g[0][None], w_in.astype(BF16), act=True, tm=512, tn=512)
    return sg_out(x2, z, ln_g[None], ln_b[None], w_sp, b_sp.T, w_out.astype(BF16), g[1][None], tm=256)


def kernel(x, norm_gains, nsa_w_in, nsa_cmp_pe, nsa_cmp_w1, nsa_cmp_w2, nsa_w_out, sg_w_in, sg_ln_g,
           sg_ln_b, sg_w_sp, sg_b_sp, sg_w_out, ffn_w_up, ffn_conv_w, ffn_conv_b, ffn_w_down):
    batch, seq, d = x.shape
    assert (seq, d) == (SEQ, D_MODEL)
    x2 = x.reshape(batch * seq, d)
    for i in range(DEPTH):
        g = norm_gains[i]
        slot = i // N_MIXERS
        if i % N_MIXERS == 0:
            x2 = _nsa_layer(x2, batch, g, nsa_w_in[slot], nsa_cmp_pe[slot], nsa_cmp_w1[slot],
                            nsa_cmp_w2[slot], nsa_w_out[slot])
        else:
            x2 = _sg_layer(x2, g, sg_w_in[slot], sg_ln_g[slot], sg_ln_b[slot], sg_w_sp[slot],
                           sg_b_sp[slot], sg_w_out[slot])
        x2 = conv_ffn(x2, g[2][None], ffn_w_up[i].astype(BF16), ffn_conv_w[i], ffn_conv_b[i][None],
                      ffn_w_down[i].astype(BF16), g[3][None], tm=512, tf=256)
    return x2.reshape(batch, seq, d)
```

```python
import functools

import numpy as np
import jax
import jax.numpy as jnp
from jax import lax
from jax.experimental import pallas as pl
from jax.experimental.pallas import tpu as pltpu

D_MODEL = 1024
SEQ = 2048
DEPTH = 4
N_MIXERS = 2
HEADS = 16
KV_GROUPS = 4
HEAD_DIM = 64
HPG = HEADS // KV_GROUPS
KV_W = KV_GROUPS * HEAD_DIM
N_BRANCH = 3
CMP_LEN = 32
CMP_STRIDE = 16
SLC_LEN = 64
SLC_TOPN = 8
WINDOW = 512
FORCE_SCORE = 1.0e4
NEG_BIG = -1.0e30
SG_WIDTH = 2 * D_MODEL
SG_GROUPS = 16
SG_GROUP_W = SG_WIDTH // SG_GROUPS
SG_CHUNK = 128
FFN_HIDDEN = 2816
CONV_W = 3
EPS = 1e-6

N_SLC = SEQ // SLC_LEN
N_CMP_PAD = SEQ // CMP_STRIDE
Q_W = HEADS * HEAD_DIM
GATE_W = N_BRANCH * HEADS

LANES = 128
BF16_SUBLANES = 16
VMEM_LIMIT = 48 * 1024 * 1024

KV_PAIR_W = KV_GROUPS * LANES
MAIN_W = Q_W + 2 * KV_PAIR_W
CMP_W = KV_PAIR_W
GATES_W = KV_GROUPS * LANES
ATT_TILE = 128
SLC_UNROLL = 4

BF16 = jnp.bfloat16
F32 = jnp.float32


def _cparams(sem):
    return pltpu.CompilerParams(dimension_semantics=sem, vmem_limit_bytes=VMEM_LIMIT)


def _rms(x, g):
    return x * lax.rsqrt(jnp.mean(x * x, axis=-1, keepdims=True) + EPS) * g


def _dot(a, b):
    return jnp.dot(a, b, preferred_element_type=F32)


def _norm_matmul_kernel(x_ref, g_ref, w_ref, o_ref, xn_ref, *, act):
    @pl.when(pl.program_id(1) == 0)
    def _():
        xn_ref[...] = _rms(x_ref[...], g_ref[...]).astype(BF16)

    acc = _dot(xn_ref[...], w_ref[...])
    if act:
        acc = jax.nn.gelu(acc)
    o_ref[...] = acc.astype(o_ref.dtype)


def norm_matmul(x, g, w, *, act, tm, tn):
    m, k = x.shape
    n = w.shape[1]
    return pl.pallas_call(
        functools.partial(_norm_matmul_kernel, act=act),
        grid=(m // tm, n // tn),
        in_specs=[
            pl.BlockSpec((tm, k), lambda i, j: (i, 0)),
            pl.BlockSpec((1, k), lambda i, j: (0, 0)),
            pl.BlockSpec((k, tn), lambda i, j: (0, j)),
        ],
        out_specs=pl.BlockSpec((tm, tn), lambda i, j: (i, j)),
        out_shape=jax.ShapeDtypeStruct((m, n), F32),
        scratch_shapes=[pltpu.VMEM((tm, k), BF16)],
        compiler_params=_cparams(("parallel", "arbitrary")),
        name="norm_matmul_gelu" if act else "norm_matmul",
    )(x, g, w)


def _nsa_in_kernel(x_ref, g_ref, w_ref, main_ref, cmp_ref, gate_ref):
    xn = _rms(x_ref[...], g_ref[...]).astype(BF16)
    main_ref[...] = _dot(xn, w_ref[:, :MAIN_W]).astype(BF16)
    cmp_ref[...] = _dot(xn, w_ref[:, MAIN_W:MAIN_W + CMP_W])
    gate_ref[...] = _dot(xn, w_ref[:, MAIN_W + CMP_W:])


def nsa_in_proj(x, g, w_all, *, tm):
    m, d = x.shape
    n = w_all.shape[1]
    return pl.pallas_call(
        _nsa_in_kernel,
        grid=(m // tm,),
        in_specs=[
            pl.BlockSpec((tm, d), lambda i: (i, 0)),
            pl.BlockSpec((1, d), lambda i: (0, 0)),
            pl.BlockSpec((d, n), lambda i: (0, 0)),
        ],
        out_specs=[
            pl.BlockSpec((tm, MAIN_W), lambda i: (i, 0)),
            pl.BlockSpec((tm, CMP_W), lambda i: (i, 0)),
            pl.BlockSpec((tm, GATES_W), lambda i: (i, 0)),
        ],
        out_shape=[
            jax.ShapeDtypeStruct((m, MAIN_W), BF16),
            jax.ShapeDtypeStruct((m, CMP_W), F32),
            jax.ShapeDtypeStruct((m, GATES_W), F32),
        ],
        compiler_params=_cparams(("parallel",)),
        name="nsa_in_proj",
    )(x, g, w_all)


def _compress_kernel(x0_ref, x1_ref, x2_ref, x3_ref, pe_ref, wa_ref, wb_ref, w2_ref, kv_ref,
                     ca_ref, cb_ref):
    nck = N_CMP_PAD
    for l in range(CMP_STRIDE):
        for g, x_ref in enumerate((x0_ref, x1_ref, x2_ref, x3_ref)):
            xg = x_ref[pl.ds(l, nck, stride=CMP_STRIDE), :]
            ca_ref[nck * g:nck * (g + 1), LANES * l:LANES * (l + 1)] = (
                xg + pe_ref[l:l + 1, :]).astype(BF16)
            cb_ref[nck * g:nck * (g + 1), LANES * l:LANES * (l + 1)] = (
                xg + pe_ref[CMP_STRIDE + l:CMP_STRIDE + l + 1, :]).astype(BF16)
    a = _dot(ca_ref[...], wa_ref[...])
    bm = _dot(cb_ref[...], wb_ref[...])
    for g in range(KV_GROUPS):
        ag = a[nck * g:nck * (g + 1)]
        bg = bm[nck * g:nck * (g + 1)]
        pre = ag + pltpu.roll(bg, nck - 1, 0)
        h = jax.nn.gelu(pre).astype(BF16)
        kv_ref[nck * g:nck * (g + 1), :] = _dot(h, w2_ref[...]).astype(BF16)


def nsa_compress(cmp3, pe_cat, wa, wb, w2bd):
    b = cmp3.shape[0]
    rows = KV_GROUPS * N_CMP_PAD
    feat = CMP_STRIDE * LANES
    x_specs = [pl.BlockSpec((None, SEQ, LANES), functools.partial(lambda i, g: (i, 0, g), g=g))
               for g in range(KV_GROUPS)]
    return pl.pallas_call(
        _compress_kernel,
        grid=(b,),
        in_specs=x_specs + [
            pl.BlockSpec((CMP_LEN, LANES), lambda i: (0, 0)),
            pl.BlockSpec((feat, LANES), lambda i: (0, 0)),
            pl.BlockSpec((feat, LANES), lambda i: (0, 0)),
            pl.BlockSpec((LANES, LANES), lambda i: (0, 0)),
        ],
        out_specs=pl.BlockSpec((None, rows, LANES), lambda i: (i, 0, 0)),
        out_shape=jax.ShapeDtypeStruct((b, rows, LANES), BF16),
        scratch_shapes=[pltpu.VMEM((rows, feat), BF16), pltpu.VMEM((rows, feat), BF16)],
        compiler_params=_cparams(("parallel",)),
        name="nsa_compress",
    )(cmp3, cmp3, cmp3, cmp3, pe_cat, wa, wb, w2bd)


def _mask_rows(s, mask, fill, tq):
    return jnp.concatenate(
        [jnp.where(mask, s[:, tq * h:tq * (h + 1)], fill) for h in range(HPG)], axis=1)


def _flash(carry, tiles, q4t, tq):
    ss = []
    for lhs, _, mask in tiles:
        s = _dot(lhs, q4t)
        ss.append(s if mask is None else _mask_rows(s, mask, NEG_BIG, tq))
    m_new = None if carry is None else carry[0]
    for s in ss:
        mx = jnp.max(s, axis=0, keepdims=True)
        m_new = mx if m_new is None else jnp.maximum(m_new, mx)
    l = acc = None
    if carry is not None:
        alpha = jnp.exp(carry[0] - m_new)
        l, acc = alpha * carry[1], alpha * carry[2]
    for s, (_, vt, _) in zip(ss, tiles):
        p = jnp.exp(s - m_new)
        ps = jnp.sum(p, axis=0, keepdims=True)
        pv = _dot(vt, p.astype(BF16))
        l = ps if l is None else l + ps
        acc = pv if acc is None else acc + pv
    return m_new, l, acc


def _nsa_attn_kernel(q_ref, kvs_ref, kvw_ref, kvc_ref, gl_ref, wt_ref, o_ref,
                     kse_ref, vst_ref, vwt_ref, vct_ref):
    tq = tk = ATT_TILE
    qi = pl.program_id(2)
    t0 = qi * tq
    win_tiles = WINDOW // tk

    @pl.when(qi == 0)
    def _():
        lane = lax.broadcasted_iota(jnp.int32, (tk, LANES), 1)
        krow = lax.broadcasted_iota(jnp.int32, (tk, LANES), 0)
        for c in range(SEQ // tk):
            kv = kvs_ref[tk * c:tk * (c + 1), :].astype(F32)
            vst_ref[c] = kv.T[HEAD_DIM:, :].astype(BF16)
            vwt_ref[c] = kvw_ref[tk * c:tk * (c + 1), :].astype(F32).T[HEAD_DIM:, :].astype(BF16)
            onehot = jnp.where(lane - HEAD_DIM == ((tk * c + krow) >> 6), 1.0, 0.0)
            kse_ref[tk * c:tk * (c + 1), :] = jnp.where(lane < HEAD_DIM, kv, onehot).astype(BF16)
        vct_ref[...] = kvc_ref[...].astype(F32).T[HEAD_DIM:, :].astype(BF16)

    qt = q_ref[...].astype(F32).T * (HEAD_DIM ** -0.5)
    zpad = jnp.zeros((HEAD_DIM, tq), F32)

    def stack_q(extra):
        return jnp.concatenate(
            [jnp.concatenate([qt[HEAD_DIM * h:HEAD_DIM * (h + 1)], extra], axis=0) for h in range(HPG)],
            axis=1).astype(BF16)

    q4t = stack_q(zpad)

    kr = lax.broadcasted_iota(jnp.int32, (tk, tq), 0)
    tc = lax.broadcasted_iota(jnp.int32, (tk, tq), 1)

    diag = kr <= tc

    tiles = []
    for u in range(win_tiles + 1):
        kt = qi - win_tiles + u
        exists = kt >= 0
        ktc = jnp.maximum(kt, 0)
        k0 = pl.multiple_of(ktc * tk, tk)
        if u == 0:
            mask = jnp.logical_and(kr > tc, exists)
        elif u < win_tiles:
            mask = jnp.logical_and(kr >= 0, exists)
        else:
            mask = diag
        tiles.append((kvw_ref[pl.ds(k0, tk), :], vwt_ref[ktc], mask))
    _, l_w, acc_w = _flash(None, tiles, q4t, tq)
    o_win = acc_w * (1.0 / l_w)

    s = _dot(kvc_ref[...], q4t)
    cmask = (kr * CMP_STRIDE + (CMP_LEN - 1)) <= (t0 + tc)
    s = _mask_rows(s, cmask, NEG_BIG, tq)
    m = jnp.max(s, axis=0, keepdims=True)
    e = _mask_rows(jnp.exp(s - m), cmask, 0.0, tq)
    den = jnp.sum(e, axis=0, keepdims=True)
    p = (e * (1.0 / jnp.maximum(den, 1e-30))).astype(BF16)
    o_cmp = _dot(vct_ref[...], p)

    imp4 = _dot(wt_ref[...], p)
    imp = imp4[:, 0:tq]
    for h in range(1, HPG):
        imp = imp + imp4[:, tq * h:tq * (h + 1)]
    j = lax.broadcasted_iota(jnp.int32, imp.shape, 0)
    blk = (t0 + lax.broadcasted_iota(jnp.int32, imp.shape, 1)) >> 6
    valid = j <= blk
    forced = (j == 0) | (j == blk) | (j == blk - 1)
    score = jnp.where(valid, imp + jnp.where(forced, FORCE_SCORE, 0.0), -FORCE_SCORE)
    rank = jnp.zeros(imp.shape, F32)
    for i in range(N_SLC):
        ri = score[i:i + 1, :]
        beats = jnp.where(ri > score, 1.0, jnp.where(ri == score, jnp.where(j > i, 1.0, 0.0), 0.0))
        rank = rank + beats
    sel_bias = jnp.where(rank < float(SLC_TOPN), 0.0, NEG_BIG)
    q4t_sel = stack_q(jnp.concatenate([sel_bias, jnp.zeros((HEAD_DIM - N_SLC, tq), F32)], axis=0))

    def slc_tile(kt, mask):
        k0 = pl.multiple_of(kt * tk, tk)
        return (kse_ref[pl.ds(k0, tk), :], vst_ref[kt], mask)

    def slc_group(gi, carry):
        return _flash(carry, [slc_tile(gi * SLC_UNROLL + u, None) for u in range(SLC_UNROLL)], q4t_sel, tq)

    lanes4 = HPG * tq
    init = (jnp.full((1, lanes4), NEG_BIG, F32), jnp.zeros((1, lanes4), F32),
            jnp.zeros((HEAD_DIM, lanes4), F32))
    n_groups = qi // SLC_UNROLL
    carry = lax.fori_loop(0, n_groups, slc_group, init)

    def slc_tail(r):
        def run(carry):
            tiles = [slc_tile(n_groups * SLC_UNROLL + u, None) for u in range(r)]
            return _flash(carry, tiles + [slc_tile(qi, diag)], q4t_sel, tq)
        return run

    _, l_s, acc_s = lax.switch(qi % SLC_UNROLL, [slc_tail(r) for r in range(SLC_UNROLL)], carry)

    o_slc = acc_s * (1.0 / l_s)
    sg = jax.nn.sigmoid(gl_ref[...].T[:BF16_SUBLANES, :])
    heads = []
    for h in range(HPG):
        sl = slice(tq * h, tq * (h + 1))
        r = N_BRANCH * h
        heads.append(sg[r:r + 1] * o_cmp[:, sl] + sg[r + 1:r + 2] * o_slc[:, sl]
                     + sg[r + 2:r + 3] * o_win[:, sl])
    o_ref[...] = jnp.concatenate(heads, axis=0).T.astype(BF16)


def nsa_attention(main3, kvc, gates, wcs_t):
    b = main3.shape[0]
    tq = ATT_TILE
    nq = SEQ // tq
    qw = HPG * HEAD_DIM
    kvs_base = Q_W // LANES
    kvw_base = kvs_base + KV_GROUPS
    return pl.pallas_call(
        _nsa_attn_kernel,
        grid=(b, KV_GROUPS, nq),
        in_specs=[
            pl.BlockSpec((None, tq, qw), lambda b_, g, i: (b_, i, g)),
            pl.BlockSpec((None, SEQ, LANES), lambda b_, g, i: (b_, 0, kvs_base + g)),
            pl.BlockSpec((None, SEQ, LANES), lambda b_, g, i: (b_, 0, kvw_base + g)),
            pl.BlockSpec((None, N_CMP_PAD, LANES), lambda b_, g, i: (b_, g, 0)),
            pl.BlockSpec((tq, LANES), lambda b_, g, i: (b_ * nq + i, g)),
            pl.BlockSpec((N_SLC, N_CMP_PAD), lambda b_, g, i: (0, 0)),
        ],
        out_specs=pl.BlockSpec((None, tq, qw), lambda b_, g, i: (b_, i, g)),
        out_shape=jax.ShapeDtypeStruct((b, SEQ, Q_W), BF16),
        scratch_shapes=[
            pltpu.VMEM((SEQ, LANES), BF16),
            pltpu.VMEM((SEQ // tq, HEAD_DIM, tq), BF16),
            pltpu.VMEM((SEQ // tq, HEAD_DIM, tq), BF16),
            pltpu.VMEM((HEAD_DIM, N_CMP_PAD), BF16),
        ],
        compiler_params=_cparams(("parallel", "parallel", "arbitrary")),
        name="nsa_attention",
    )(main3, main3, main3, kvc, gates, wcs_t)


def _proj_norm_res_kernel(x_ref, a_ref, w_ref, g_ref, out_ref):
    out_ref[...] = x_ref[...] + _rms(_dot(a_ref[...], w_ref[...]), g_ref[...])


def proj_norm_res(x, a, w, g, *, tm):
    m, d = x.shape
    k = a.shape[1]
    row = pl.BlockSpec((tm, d), lambda i: (i, 0))
    return pl.pallas_call(
        _proj_norm_res_kernel,
        grid=(m // tm,),
        in_specs=[row, pl.BlockSpec((tm, k), lambda i: (i, 0)), pl.BlockSpec((k, d), lambda i: (0, 0)),
                  pl.BlockSpec((1, d), lambda i: (0, 0))],
        out_specs=row,
        out_shape=jax.ShapeDtypeStruct((m, d), F32),
        compiler_params=_cparams(("parallel",)),
        name="proj_norm_res",
    )(x, a, w, g)


def _sg_out_kernel(x_ref, u_ref, v_ref, lng_ref, lnb_ref, wsp_ref, bt_ref, wo_ref, g1_ref,
                   out_ref, wm_ref, gated_ref, *, tm):
    @pl.when(pl.program_id(0) == 0)
    def _():
        t = lax.broadcasted_iota(jnp.int32, (SG_CHUNK, SG_CHUNK), 0)
        s = lax.broadcasted_iota(jnp.int32, (SG_CHUNK, SG_CHUNK), 1)
        for g in range(SG_GROUPS):
            wm_ref[g] = jnp.where(s <= t, wsp_ref[g], 0.0).astype(BF16)

    v = v_ref[...]
    mu = jnp.mean(v, axis=-1, keepdims=True)
    vc = v - mu
    var = jnp.mean(vc * vc, axis=-1, keepdims=True)
    vn = (vc * lax.rsqrt(var + EPS) * lng_ref[...] + lnb_ref[...]).astype(BF16)
    for c in range(tm // SG_CHUNK):
        r0, r1 = SG_CHUNK * c, SG_CHUNK * (c + 1)
        for g in range(SG_GROUPS):
            c0, c1 = SG_GROUP_W * g, SG_GROUP_W * (g + 1)
            mixed = _dot(wm_ref[g], vn[r0:r1, c0:c1]) + bt_ref[:, g:g + 1]
            gated_ref[r0:r1, c0:c1] = (u_ref[r0:r1, c0:c1] * mixed).astype(BF16)
    m = _dot(gated_ref[...], wo_ref[...])
    out_ref[...] = x_ref[...] + _rms(m, g1_ref[...])


def sg_out(x, z, ln_g, ln_b, w_sp, b_sp_t, w_out, g1, *, tm):
    m, d = x.shape
    e = SG_WIDTH
    row = pl.BlockSpec((tm, d), lambda i: (i, 0))
    return pl.pallas_call(
        functools.partial(_sg_out_kernel, tm=tm),
        grid=(m // tm,),
        in_specs=[
            row,
            pl.BlockSpec((tm, e), lambda i: (i, 0)),
            pl.BlockSpec((tm, e), lambda i: (i, 1)),
            pl.BlockSpec((1, e), lambda i: (0, 0)),
            pl.BlockSpec((1, e), lambda i: (0, 0)),
            pl.BlockSpec((SG_GROUPS, SG_CHUNK, SG_CHUNK), lambda i: (0, 0, 0)),
            pl.BlockSpec((SG_CHUNK, SG_GROUPS), lambda i: (0, 0)),
            pl.BlockSpec((e, d), lambda i: (0, 0)),
            pl.BlockSpec((1, d), lambda i: (0, 0)),
        ],
        out_specs=row,
        out_shape=jax.ShapeDtypeStruct((m, d), F32),
        scratch_shapes=[pltpu.VMEM((SG_GROUPS, SG_CHUNK, SG_CHUNK), BF16), pltpu.VMEM((tm, e), BF16)],
        compiler_params=_cparams(("arbitrary",)),
        name="sg_out",
    )(x, z, z, ln_g, ln_b, w_sp, b_sp_t, w_out, g1)


HALO = BF16_SUBLANES


def _ffn_kernel(x_ref, xh_ref, g2_ref, wg_ref, wv_ref, cwg_ref, cwv_ref, cbg_ref, cbv_ref,
                wd_ref, g3_ref, out_ref, xn_ref, acc_ref, hg_ref, hv_ref, *, tm, tiles_per_seq):
    i = pl.program_id(0)
    j = pl.program_id(1)

    @pl.when(j == 0)
    def _():
        xn_ref[HALO:, :] = _rms(x_ref[...], g2_ref[...]).astype(BF16)
        hn = _rms(xh_ref[...], g2_ref[...])
        seq_start = (i % tiles_per_seq) == 0
        xn_ref[:HALO, :] = jnp.where(seq_start, 0.0, hn).astype(BF16)
        acc_ref[...] = jnp.zeros_like(acc_ref)

    xa = xn_ref[...]
    hg_ref[...] = _dot(xa, wg_ref[...])
    hv_ref[...] = _dot(xa, wv_ref[...])

    def conv(h_ref, cw_ref, cb_ref):
        out = cb_ref[...] + cw_ref[CONV_W - 1:CONV_W, :] * h_ref[HALO:, :]
        for k in range(CONV_W - 1):
            off = HALO - (CONV_W - 1 - k)
            out = out + cw_ref[k:k + 1, :] * h_ref[off:off + tm, :]
        return out

    cg = conv(hg_ref, cwg_ref, cbg_ref)
    cv = conv(hv_ref, cwv_ref, cbv_ref)
    act = (jax.nn.silu(cg) * cv).astype(BF16)
    acc_ref[...] += _dot(act, wd_ref[...])

    @pl.when(j == pl.num_programs(1) - 1)
    def _():
        out_ref[...] = x_ref[...] + _rms(acc_ref[...], g3_ref[...])


def conv_ffn(x, g2, w_up, conv_w, conv_b, w_down, g3, *, tm, tf):
    m, d = x.shape
    f = FFN_HIDDEN
    nf = f // tf
    tiles_per_seq = SEQ // tm
    halo_blocks = tm // HALO
    vec = pl.BlockSpec((1, d), lambda i, j: (0, 0))
    return pl.pallas_call(
        functools.partial(_ffn_kernel, tm=tm, tiles_per_seq=tiles_per_seq),
        grid=(m // tm, nf),
        in_specs=[
            pl.BlockSpec((tm, d), lambda i, j: (i, 0)),
            pl.BlockSpec((HALO, d), lambda i, j: (jnp.maximum(i * halo_blocks - 1, 0), 0)),
            vec,
            pl.BlockSpec((d, tf), lambda i, j: (0, j)),
            pl.BlockSpec((d, tf), lambda i, j: (0, nf + j)),
            pl.BlockSpec((CONV_W, tf), lambda i, j: (0, j)),
            pl.BlockSpec((CONV_W, tf), lambda i, j: (0, nf + j)),
            pl.BlockSpec((1, tf), lambda i, j: (0, j)),
            pl.BlockSpec((1, tf), lambda i, j: (0, nf + j)),
            pl.BlockSpec((tf, d), lambda i, j: (j, 0)),
            vec,
        ],
        out_specs=pl.BlockSpec((tm, d), lambda i, j: (i, 0)),
        out_shape=jax.ShapeDtypeStruct((m, d), F32),
        scratch_shapes=[
            pltpu.VMEM((HALO + tm, d), BF16),
            pltpu.VMEM((tm, d), F32),
            pltpu.VMEM((HALO + tm, tf), F32),
            pltpu.VMEM((HALO + tm, tf), F32),
        ],
        compiler_params=_cparams(("parallel", "arbitrary")),
        name="conv_ffn",
    )(x, x, g2, w_up, w_up, conv_w, conv_w, conv_b, conv_b, w_down, g3)


def _kv_pair_columns(pair):
    k_base = Q_W + (2 * pair) * KV_W
    v_base = Q_W + (2 * pair + 1) * KV_W
    order = []
    for g in range(KV_GROUPS):
        order += list(range(k_base + HEAD_DIM * g, k_base + HEAD_DIM * (g + 1)))
        order += list(range(v_base + HEAD_DIM * g, v_base + HEAD_DIM * (g + 1)))
    return order


def _nsa_in_weights(w_in):
    main = list(range(Q_W)) + _kv_pair_columns(1) + _kv_pair_columns(2)
    order = np.asarray(main + _kv_pair_columns(0), dtype=np.int32)
    w_gl = w_in[:, Q_W + 6 * KV_W:].reshape(D_MODEL, KV_GROUPS, HPG * N_BRANCH)
    w_gl = jnp.pad(w_gl, ((0, 0), (0, 0), (0, LANES - HPG * N_BRANCH))).reshape(D_MODEL, GATES_W)
    return jnp.concatenate([jnp.take(w_in, order, axis=1), w_gl], axis=1).astype(BF16)


def _cmp_to_slc_t():
    n_cmp = (SEQ - CMP_LEN) // CMP_STRIDE + 1
    cs = np.arange(n_cmp)[:, None] * CMP_STRIDE
    ss = np.arange(N_SLC)[None, :] * SLC_LEN
    ov = np.clip(np.minimum(cs + CMP_LEN, ss + SLC_LEN) - np.maximum(cs, ss), 0, None)
    w = ov.astype(np.float32) / np.float32(CMP_LEN)
    wt = np.zeros((N_SLC, N_CMP_PAD), dtype=np.float32)
    wt[:, :n_cmp] = w.T
    return jnp.asarray(wt, dtype=BF16)


def _block_diag2(a, b):
    za = jnp.zeros((a.shape[0], b.shape[1]), a.dtype)
    zb = jnp.zeros((b.shape[0], a.shape[1]), a.dtype)
    return jnp.concatenate([jnp.concatenate([a, za], axis=1), jnp.concatenate([zb, b], axis=1)], axis=0)


def _nsa_layer(x2, batch, g, w_in, cmp_pe, cmp_w1, cmp_w2, w_out):
    w1 = cmp_w1.reshape(2, CMP_LEN, HEAD_DIM, HEAD_DIM)
    w1bd = jnp.stack([_block_diag2(w1[0, l], w1[1, l]) for l in range(CMP_LEN)])
    wa = w1bd[:CMP_STRIDE].reshape(CMP_STRIDE * LANES, LANES).astype(BF16)
    wb = w1bd[CMP_STRIDE:].reshape(CMP_STRIDE * LANES, LANES).astype(BF16)
    w2bd = _block_diag2(cmp_w2[0], cmp_w2[1]).astype(BF16)
    pe_cat = jnp.concatenate([cmp_pe[0], cmp_pe[1]], axis=1)

    main, cmp, gates = nsa_in_proj(x2, g[0][None], _nsa_in_weights(w_in), tm=512)
    kvc = nsa_compress(cmp.reshape(batch, SEQ, CMP_W), pe_cat, wa, wb, w2bd)
    o = nsa_attention(main.reshape(batch, SEQ, MAIN_W), kvc, gates, _cmp_to_slc_t())
    return proj_norm_res(x2, o.reshape(-1, Q_W), w_out.astype(BF16), g[1][None], tm=512)


def _sg_layer(x2, g, w_in, ln_g, ln_b, w_sp, b_sp, w_out):
    z = norm_matmul(x2, g[0][None], w_in.astype(BF16), act=True, tm=512, tn=512)
    return sg_out(x2, z, ln_g[None], ln_b[None], w_sp, b_sp.T, w_out.astype(BF16), g[1][None], tm=256)


def kernel(x, norm_gains, nsa_w_in, nsa_cmp_pe, nsa_cmp_w1, nsa_cmp_w2, nsa_w_out, sg_w_in, sg_ln_g,
           sg_ln_b, sg_w_sp, sg_b_sp, sg_w_out, ffn_w_up, ffn_conv_w, ffn_conv_b, ffn_w_down):
    batch, seq, d = x.shape
    assert (seq, d) == (SEQ, D_MODEL)
    x2 = x.reshape(batch * seq, d)
    for i in range(DEPTH):
        g = norm_gains[i]
        slot = i // N_MIXERS
        if i % N_MIXERS == 0:
            x2 = _nsa_layer(x2, batch, g, nsa_w_in[slot], nsa_cmp_pe[slot], nsa_cmp_w1[slot],
                            nsa_cmp_w2[slot], nsa_w_out[slot])
        else:
            x2 = _sg_layer(x2, g, sg_w_in[slot], sg_ln_g[slot], sg_ln_b[slot], sg_w_sp[slot],
                           sg_b_sp[slot], sg_w_out[slot])
        x2 = conv_ffn(x2, g[2][None], ffn_w_up[i].astype(BF16), ffn_conv_w[i], ffn_conv_b[i][None],
                      ffn_w_down[i].astype(BF16), g[3][None], tm=512, tf=256)
    return x2.reshape(batch, seq, d)
```

```python
import functools

import numpy as np
import jax
import jax.numpy as jnp
from jax import lax
from jax.experimental import pallas as pl
from jax.experimental.pallas import tpu as pltpu

D_MODEL = 1024
SEQ = 2048
DEPTH = 4
N_MIXERS = 2
HEADS = 16
KV_GROUPS = 4
HEAD_DIM = 64
HPG = HEADS // KV_GROUPS
KV_W = KV_GROUPS * HEAD_DIM
N_BRANCH = 3
CMP_LEN = 32
CMP_STRIDE = 16
SLC_LEN = 64
SLC_TOPN = 8
WINDOW = 512
FORCE_SCORE = 1.0e4
NEG_BIG = -1.0e30
SG_WIDTH = 2 * D_MODEL
SG_GROUPS = 16
SG_GROUP_W = SG_WIDTH // SG_GROUPS
SG_CHUNK = 128
FFN_HIDDEN = 2816
CONV_W = 3
EPS = 1e-6

N_SLC = SEQ // SLC_LEN
N_CMP_PAD = SEQ // CMP_STRIDE
Q_W = HEADS * HEAD_DIM
GATE_W = N_BRANCH * HEADS

LANES = 128
F32_SUBLANES = 8
BF16_SUBLANES = 16
VMEM_LIMIT = 48 * 1024 * 1024

KV_PAIR_W = KV_GROUPS * LANES
MAIN_W = Q_W + 2 * KV_PAIR_W
CMP_W = KV_PAIR_W
GATES_W = KV_GROUPS * LANES
ATT_TILE = 128
SLC_UNROLL = 4
ATT_GROUPS = 2

BF16 = jnp.bfloat16
F32 = jnp.float32


def _cparams(sem):
    return pltpu.CompilerParams(dimension_semantics=sem, vmem_limit_bytes=VMEM_LIMIT)


def _rms(x, g):
    return x * lax.rsqrt(jnp.mean(x * x, axis=-1, keepdims=True) + EPS) * g


def _dot(a, b):
    return jnp.dot(a, b, preferred_element_type=F32)


def _norm_matmul_gelu_kernel(x_ref, g_ref, w_ref, o_ref, *, chunk):
    xn = _rms(x_ref[...], g_ref[...]).astype(BF16)
    for c in range(w_ref.shape[1] // chunk):
        o_ref[:, chunk * c:chunk * (c + 1)] = jax.nn.gelu(_dot(xn, w_ref[:, chunk * c:chunk * (c + 1)]))


def norm_matmul_gelu(x, g, w, *, tm, chunk):
    m, k = x.shape
    n = w.shape[1]
    return pl.pallas_call(
        functools.partial(_norm_matmul_gelu_kernel, chunk=chunk),
        grid=(m // tm,),
        in_specs=[
            pl.BlockSpec((tm, k), lambda i: (i, 0)),
            pl.BlockSpec((1, k), lambda i: (0, 0)),
            pl.BlockSpec((k, n), lambda i: (0, 0)),
        ],
        out_specs=pl.BlockSpec((tm, n), lambda i: (i, 0)),
        out_shape=jax.ShapeDtypeStruct((m, n), F32),
        compiler_params=_cparams(("parallel",)),
        name="norm_matmul_gelu",
    )(x, g, w)


def _nsa_in_kernel(x_ref, g_ref, w_ref, main_ref, cmp_ref, gate_ref):
    xn = _rms(x_ref[...], g_ref[...]).astype(BF16)
    main_ref[...] = _dot(xn, w_ref[:, :MAIN_W]).astype(BF16)
    cmp_ref[...] = _dot(xn, w_ref[:, MAIN_W:MAIN_W + CMP_W])
    gate_ref[...] = _dot(xn, w_ref[:, MAIN_W + CMP_W:])


def nsa_in_proj(x, g, w_all, *, tm):
    m, d = x.shape
    n = w_all.shape[1]
    return pl.pallas_call(
        _nsa_in_kernel,
        grid=(m // tm,),
        in_specs=[
            pl.BlockSpec((tm, d), lambda i: (i, 0)),
            pl.BlockSpec((1, d), lambda i: (0, 0)),
            pl.BlockSpec((d, n), lambda i: (0, 0)),
        ],
        out_specs=[
            pl.BlockSpec((tm, MAIN_W), lambda i: (i, 0)),
            pl.BlockSpec((tm, CMP_W), lambda i: (i, 0)),
            pl.BlockSpec((tm, GATES_W), lambda i: (i, 0)),
        ],
        out_shape=[
            jax.ShapeDtypeStruct((m, MAIN_W), BF16),
            jax.ShapeDtypeStruct((m, CMP_W), F32),
            jax.ShapeDtypeStruct((m, GATES_W), F32),
        ],
        compiler_params=_cparams(("parallel",)),
        name="nsa_in_proj",
    )(x, g, w_all)


def _compress_kernel(x0_ref, x1_ref, x2_ref, x3_ref, pe_ref, wa_ref, wb_ref, w2_ref, kv_ref,
                     ca_ref, cb_ref):
    nck = N_CMP_PAD
    for l in range(CMP_STRIDE):
        for g, x_ref in enumerate((x0_ref, x1_ref, x2_ref, x3_ref)):
            xg = x_ref[pl.ds(l, nck, stride=CMP_STRIDE), :]
            ca_ref[nck * g:nck * (g + 1), LANES * l:LANES * (l + 1)] = (
                xg + pe_ref[l:l + 1, :]).astype(BF16)
            cb_ref[nck * g:nck * (g + 1), LANES * l:LANES * (l + 1)] = (
                xg + pe_ref[CMP_STRIDE + l:CMP_STRIDE + l + 1, :]).astype(BF16)
    a = _dot(ca_ref[...], wa_ref[...])
    bm = _dot(cb_ref[...], wb_ref[...])
    for g in range(KV_GROUPS):
        ag = a[nck * g:nck * (g + 1)]
        bg = bm[nck * g:nck * (g + 1)]
        pre = ag + pltpu.roll(bg, nck - 1, 0)
        h = jax.nn.gelu(pre).astype(BF16)
        kv_ref[nck * g:nck * (g + 1), :] = _dot(h, w2_ref[...]).astype(BF16)


def nsa_compress(cmp3, pe_cat, wa, wb, w2bd):
    b = cmp3.shape[0]
    rows = KV_GROUPS * N_CMP_PAD
    feat = CMP_STRIDE * LANES
    x_specs = [pl.BlockSpec((None, SEQ, LANES), functools.partial(lambda i, g: (i, 0, g), g=g))
               for g in range(KV_GROUPS)]
    return pl.pallas_call(
        _compress_kernel,
        grid=(b,),
        in_specs=x_specs + [
            pl.BlockSpec((CMP_LEN, LANES), lambda i: (0, 0)),
            pl.BlockSpec((feat, LANES), lambda i: (0, 0)),
            pl.BlockSpec((feat, LANES), lambda i: (0, 0)),
            pl.BlockSpec((LANES, LANES), lambda i: (0, 0)),
        ],
        out_specs=pl.BlockSpec((None, rows, LANES), lambda i: (i, 0, 0)),
        out_shape=jax.ShapeDtypeStruct((b, rows, LANES), BF16),
        scratch_shapes=[pltpu.VMEM((rows, feat), BF16), pltpu.VMEM((rows, feat), BF16)],
        compiler_params=_cparams(("parallel",)),
        name="nsa_compress",
    )(cmp3, cmp3, cmp3, cmp3, pe_cat, wa, wb, w2bd)


def _mask_rows(s, mask, fill, tq):
    return jnp.concatenate(
        [jnp.where(mask, s[:, tq * h:tq * (h + 1)], fill) for h in range(HPG)], axis=1)


def _flash(carry, tiles, q4t, tq):
    ss = []
    for lhs, _, mask in tiles:
        s = _dot(lhs, q4t)
        ss.append(s if mask is None else _mask_rows(s, mask, NEG_BIG, tq))
    m_new = None if carry is None else carry[0]
    for s in ss:
        mx = jnp.max(s, axis=0, keepdims=True)
        m_new = mx if m_new is None else jnp.maximum(m_new, mx)
    l = acc = None
    if carry is not None:
        alpha = jnp.exp(carry[0] - m_new)
        l, acc = alpha * carry[1], alpha * carry[2]
    for s, (_, vt, _) in zip(ss, tiles):
        p = jnp.exp(s - m_new)
        ps = jnp.sum(p, axis=0, keepdims=True)
        pv = _dot(vt, p.astype(BF16))
        l = ps if l is None else l + ps
        acc = pv if acc is None else acc + pv
    return m_new, l, acc


def _attn_group_prologue(qi, q_ref, kvw_ref, kvc_ref, wt_ref, vwt_ref, vct_ref):
    tq = tk = ATT_TILE
    t0 = qi * tq
    win_tiles = WINDOW // tk

    qt = q_ref[...].astype(F32).T * (HEAD_DIM ** -0.5)
    zpad = jnp.zeros((HEAD_DIM, tq), F32)

    def stack_q(extra):
        return jnp.concatenate(
            [jnp.concatenate([qt[HEAD_DIM * h:HEAD_DIM * (h + 1)], extra], axis=0) for h in range(HPG)],
            axis=1).astype(BF16)

    q4t = stack_q(zpad)

    kr = lax.broadcasted_iota(jnp.int32, (tk, tq), 0)
    tc = lax.broadcasted_iota(jnp.int32, (tk, tq), 1)

    diag = kr <= tc

    tiles = []
    for u in range(win_tiles + 1):
        kt = qi - win_tiles + u
        exists = kt >= 0
        ktc = jnp.maximum(kt, 0)
        k0 = pl.multiple_of(ktc * tk, tk)
        if u == 0:
            mask = jnp.logical_and(kr > tc, exists)
        elif u < win_tiles:
            mask = jnp.logical_and(kr >= 0, exists)
        else:
            mask = diag
        tiles.append((kvw_ref[pl.ds(k0, tk), :], vwt_ref[ktc], mask))
    _, l_w, acc_w = _flash(None, tiles, q4t, tq)
    o_win = acc_w * (1.0 / l_w)

    s = _dot(kvc_ref[...], q4t)
    cmask = (kr * CMP_STRIDE + (CMP_LEN - 1)) <= (t0 + tc)
    s = _mask_rows(s, cmask, NEG_BIG, tq)
    m = jnp.max(s, axis=0, keepdims=True)
    e = _mask_rows(jnp.exp(s - m), cmask, 0.0, tq)
    den = jnp.sum(e, axis=0, keepdims=True)
    p = (e * (1.0 / jnp.maximum(den, 1e-30))).astype(BF16)
    o_cmp = _dot(vct_ref[...], p)

    imp4 = _dot(wt_ref[...], p)
    imp = imp4[:, 0:tq]
    for h in range(1, HPG):
        imp = imp + imp4[:, tq * h:tq * (h + 1)]
    j = lax.broadcasted_iota(jnp.int32, imp.shape, 0)
    blk = (t0 + lax.broadcasted_iota(jnp.int32, imp.shape, 1)) >> 6
    valid = j <= blk
    forced = (j == 0) | (j == blk) | (j == blk - 1)
    score = jnp.where(valid, imp + jnp.where(forced, FORCE_SCORE, 0.0), -FORCE_SCORE)
    rank = jnp.zeros(imp.shape, F32)
    for i in range(N_SLC):
        ri = score[i:i + 1, :]
        beats = jnp.where(ri > score, 1.0, jnp.where(ri == score, jnp.where(j > i, 1.0, 0.0), 0.0))
        rank = rank + beats
    sel_bias = jnp.where(rank < float(SLC_TOPN), 0.0, NEG_BIG)
    q4t_sel = stack_q(jnp.concatenate([sel_bias, jnp.zeros((HEAD_DIM - N_SLC, tq), F32)], axis=0))
    return q4t_sel, o_win, o_cmp


def _nsa_attn_kernel(*refs):
    ng = ATT_GROUPS
    q_ref = refs[0]
    kvs_refs = refs[1:1 + ng]
    kvw_refs = refs[1 + ng:1 + 2 * ng]
    kvc_ref, gl_ref, wt_ref, o_ref, kse_ref, vst_ref, vwt_ref, vct_ref = refs[1 + 2 * ng:]
    tq = tk = ATT_TILE
    qw = HPG * HEAD_DIM
    qi = pl.program_id(2)

    @pl.when(qi == 0)
    def _():
        lane = lax.broadcasted_iota(jnp.int32, (tk, LANES), 1)
        krow = lax.broadcasted_iota(jnp.int32, (tk, LANES), 0)
        for gg in range(ng):
            for c in range(SEQ // tk):
                kv = kvs_refs[gg][tk * c:tk * (c + 1), :].astype(F32)
                vst_ref[gg, c] = kv.T[HEAD_DIM:, :].astype(BF16)
                vwt_ref[gg, c] = kvw_refs[gg][tk * c:tk * (c + 1), :].astype(F32).T[HEAD_DIM:, :].astype(BF16)
                onehot = jnp.where(lane - HEAD_DIM == ((tk * c + krow) >> 6), 1.0, 0.0)
                kse_ref[gg, tk * c:tk * (c + 1), :] = jnp.where(lane < HEAD_DIM, kv, onehot).astype(BF16)
            vct_ref[gg] = kvc_ref[N_CMP_PAD * gg:N_CMP_PAD * (gg + 1), :].astype(F32).T[HEAD_DIM:, :].astype(BF16)

    pre = [_attn_group_prologue(qi, q_ref.at[:, qw * gg:qw * (gg + 1)], kvw_refs[gg],
                                kvc_ref.at[N_CMP_PAD * gg:N_CMP_PAD * (gg + 1), :], wt_ref,
                                vwt_ref.at[gg], vct_ref.at[gg]) for gg in range(ng)]

    kr = lax.broadcasted_iota(jnp.int32, (tk, tq), 0)
    tc = lax.broadcasted_iota(jnp.int32, (tk, tq), 1)
    diag = kr <= tc

    def slc_tile(gg, kt, mask):
        k0 = pl.multiple_of(kt * tk, tk)
        return (kse_ref[gg, pl.ds(k0, tk), :], vst_ref[gg, kt], mask)

    def slc_group(gi, carries):
        return tuple(
            _flash(carries[gg], [slc_tile(gg, gi * SLC_UNROLL + u, None) for u in range(SLC_UNROLL)],
                   pre[gg][0], tq) for gg in range(ng))

    lanes4 = HPG * tq
    init = (jnp.full((1, lanes4), NEG_BIG, F32), jnp.zeros((1, lanes4), F32),
            jnp.zeros((HEAD_DIM, lanes4), F32))
    n_groups = qi // SLC_UNROLL
    carries = lax.fori_loop(0, n_groups, slc_group, (init,) * ng)

    def slc_tail(r):
        def run(carries):
            out = []
            for gg in range(ng):
                tiles = [slc_tile(gg, n_groups * SLC_UNROLL + u, None) for u in range(r)]
                out.append(_flash(carries[gg], tiles + [slc_tile(gg, qi, diag)], pre[gg][0], tq))
            return tuple(out)
        return run

    carries = lax.switch(qi % SLC_UNROLL, [slc_tail(r) for r in range(SLC_UNROLL)], carries)

    for gg in range(ng):
        _, l_s, acc_s = carries[gg]
        _, o_win, o_cmp = pre[gg]
        o_slc = acc_s * (1.0 / l_s)
        sg = jax.nn.sigmoid(gl_ref[:, LANES * gg:LANES * (gg + 1)].T[:BF16_SUBLANES, :])
        heads = []
        for h in range(HPG):
            sl = slice(tq * h, tq * (h + 1))
            r = N_BRANCH * h
            heads.append(sg[r:r + 1] * o_cmp[:, sl] + sg[r + 1:r + 2] * o_slc[:, sl]
                         + sg[r + 2:r + 3] * o_win[:, sl])
        o_ref[:, qw * gg:qw * (gg + 1)] = jnp.concatenate(heads, axis=0).T.astype(BF16)


def nsa_attention(main3, kvc, gates, wcs_t):
    b = main3.shape[0]
    ng = ATT_GROUPS
    tq = ATT_TILE
    nq = SEQ // tq
    qw = ng * HPG * HEAD_DIM
    kvs_base = Q_W // LANES
    kvw_base = kvs_base + KV_GROUPS

    def kv_spec(base, gg):
        return pl.BlockSpec((None, SEQ, LANES), lambda b_, gp, i: (b_, 0, base + gp * ng + gg))

    return pl.pallas_call(
        _nsa_attn_kernel,
        grid=(b, KV_GROUPS // ng, nq),
        in_specs=[pl.BlockSpec((None, tq, qw), lambda b_, gp, i: (b_, i, gp))]
        + [kv_spec(kvs_base, gg) for gg in range(ng)]
        + [kv_spec(kvw_base, gg) for gg in range(ng)]
        + [
            pl.BlockSpec((None, ng * N_CMP_PAD, LANES), lambda b_, gp, i: (b_, gp, 0)),
            pl.BlockSpec((tq, ng * LANES), lambda b_, gp, i: (b_ * nq + i, gp)),
            pl.BlockSpec((N_SLC, N_CMP_PAD), lambda b_, gp, i: (0, 0)),
        ],
        out_specs=pl.BlockSpec((None, tq, qw), lambda b_, gp, i: (b_, i, gp)),
        out_shape=jax.ShapeDtypeStruct((b, SEQ, Q_W), BF16),
        scratch_shapes=[
            pltpu.VMEM((ng, SEQ, LANES), BF16),
            pltpu.VMEM((ng, SEQ // tq, HEAD_DIM, tq), BF16),
            pltpu.VMEM((ng, SEQ // tq, HEAD_DIM, tq), BF16),
            pltpu.VMEM((ng, HEAD_DIM, N_CMP_PAD), BF16),
        ],
        compiler_params=_cparams(("parallel", "parallel", "arbitrary")),
        name="nsa_attention",
    )(main3, *([main3] * (2 * ng)), kvc, gates, wcs_t)


def _proj_norm_res_kernel(x_ref, a_ref, w_ref, g_ref, out_ref):
    out_ref[...] = x_ref[...] + _rms(_dot(a_ref[...], w_ref[...]), g_ref[...])


def proj_norm_res(x, a, w, g, *, tm):
    m, d = x.shape
    k = a.shape[1]
    row = pl.BlockSpec((tm, d), lambda i: (i, 0))
    return pl.pallas_call(
        _proj_norm_res_kernel,
        grid=(m // tm,),
        in_specs=[row, pl.BlockSpec((tm, k), lambda i: (i, 0)), pl.BlockSpec((k, d), lambda i: (0, 0)),
                  pl.BlockSpec((1, d), lambda i: (0, 0))],
        out_specs=row,
        out_shape=jax.ShapeDtypeStruct((m, d), F32),
        compiler_params=_cparams(("parallel",)),
        name="proj_norm_res",
    )(x, a, w, g)


def _sg_out_kernel(x_ref, u_ref, v_ref, lng_ref, lnb_ref, wsp_ref, bt_ref, wo_ref, g1_ref,
                   out_ref, wm_ref, gated_ref, *, tm):
    @pl.when(pl.program_id(0) == 0)
    def _():
        t = lax.broadcasted_iota(jnp.int32, (SG_CHUNK, SG_CHUNK), 0)
        s = lax.broadcasted_iota(jnp.int32, (SG_CHUNK, SG_CHUNK), 1)
        for g in range(SG_GROUPS):
            wm_ref[g] = jnp.where(s <= t, wsp_ref[g], 0.0).astype(BF16)

    v = v_ref[...]
    mu = jnp.mean(v, axis=-1, keepdims=True)
    vc = v - mu
    var = jnp.mean(vc * vc, axis=-1, keepdims=True)
    vn = (vc * lax.rsqrt(var + EPS) * lng_ref[...] + lnb_ref[...]).astype(BF16)
    for c in range(tm // SG_CHUNK):
        r0, r1 = SG_CHUNK * c, SG_CHUNK * (c + 1)
        for g in range(SG_GROUPS):
            c0, c1 = SG_GROUP_W * g, SG_GROUP_W * (g + 1)
            mixed = _dot(wm_ref[g], vn[r0:r1, c0:c1]) + bt_ref[:, g:g + 1]
            gated_ref[r0:r1, c0:c1] = (u_ref[r0:r1, c0:c1] * mixed).astype(BF16)
    m = _dot(gated_ref[...], wo_ref[...])
    out_ref[...] = x_ref[...] + _rms(m, g1_ref[...])


def sg_out(x, z, ln_g, ln_b, w_sp, b_sp_t, w_out, g1, *, tm):
    m, d = x.shape
    e = SG_WIDTH
    row = pl.BlockSpec((tm, d), lambda i: (i, 0))
    return pl.pallas_call(
        functools.partial(_sg_out_kernel, tm=tm),
        grid=(m // tm,),
        in_specs=[
            row,
            pl.BlockSpec((tm, e), lambda i: (i, 0)),
            pl.BlockSpec((tm, e), lambda i: (i, 1)),
            pl.BlockSpec((1, e), lambda i: (0, 0)),
            pl.BlockSpec((1, e), lambda i: (0, 0)),
            pl.BlockSpec((SG_GROUPS, SG_CHUNK, SG_CHUNK), lambda i: (0, 0, 0)),
            pl.BlockSpec((SG_CHUNK, SG_GROUPS), lambda i: (0, 0)),
            pl.BlockSpec((e, d), lambda i: (0, 0)),
            pl.BlockSpec((1, d), lambda i: (0, 0)),
        ],
        out_specs=row,
        out_shape=jax.ShapeDtypeStruct((m, d), F32),
        scratch_shapes=[pltpu.VMEM((SG_GROUPS, SG_CHUNK, SG_CHUNK), BF16), pltpu.VMEM((tm, e), BF16)],
        compiler_params=_cparams(("arbitrary",)),
        name="sg_out",
    )(x, z, z, ln_g, ln_b, w_sp, b_sp_t, w_out, g1)


HALO = BF16_SUBLANES
FFN_CHUNKS = tuple((b, min(b + 4, FFN_HIDDEN // LANES)) for b in range(0, FFN_HIDDEN // LANES, 4))


def _ffn_kernel(x_ref, xh_ref, g2_ref, wg_ref, wv_ref, cwg_ref, cwv_ref, cbg_ref, cbv_ref,
                wd_ref, g3_ref, out_ref, xn_ref, y_ref, hg0_ref, hg1_ref, hv0_ref, hv1_ref,
                *, tm, tiles_per_seq, chunks):
    i = pl.program_id(0)
    phase_rows = tm // F32_SUBLANES
    d_blocks = D_MODEL // LANES

    xn_ref[HALO:, :] = _rms(x_ref[...], g2_ref[...]).astype(BF16)
    hn = _rms(xh_ref[...], g2_ref[...])
    seq_start = (i % tiles_per_seq) == 0
    xn_ref[:HALO, :] = jnp.where(seq_start, 0.0, hn).astype(BF16)

    hg_refs, hv_refs = (hg0_ref, hg1_ref), (hv0_ref, hv1_ref)

    def phases(h_ref, slot):
        return [h_ref[slot, pl.ds(HALO + p, phase_rows, stride=F32_SUBLANES), :]
                for p in range(1 - CONV_W, F32_SUBLANES)]

    def conv(ph, cw_ref, cb_ref, b, s):
        lo, hi = LANES * b, LANES * (b + 1)
        out = cb_ref[:, lo:hi]
        for k in range(CONV_W):
            out = out + cw_ref[k:k + 1, lo:hi] * ph[s + k]
        return out

    xa = xn_ref[...]

    def up_proj(n):
        b0, b1 = chunks[n]
        c0, c1 = LANES * b0, LANES * b1
        hg = _dot(xa, wg_ref[:, c0:c1])
        hv = _dot(xa, wv_ref[:, c0:c1])
        for b in range(b0, b1):
            hg_refs[n % 2][b - b0] = hg[:, LANES * (b - b0):LANES * (b - b0 + 1)]
            hv_refs[n % 2][b - b0] = hv[:, LANES * (b - b0):LANES * (b - b0 + 1)]

    def conv_act_down(n):
        b0, b1 = chunks[n]
        cols = []
        for b in range(b0, b1):
            pg = phases(hg_refs[n % 2], b - b0)
            pv = phases(hv_refs[n % 2], b - b0)
            rows = []
            for s in range(F32_SUBLANES):
                cg = conv(pg, cwg_ref, cbg_ref, b, s)
                cv = conv(pv, cwv_ref, cbv_ref, b, s)
                rows.append((jax.nn.silu(cg) * cv).astype(BF16))
            cols.append(jnp.concatenate(rows, axis=0))
        act = jnp.concatenate(cols, axis=1)
        return _dot(act, wd_ref[LANES * b0:LANES * b1, :])

    acc = None
    up_proj(0)
    for n in range(len(chunks)):
        if n + 1 < len(chunks):
            up_proj(n + 1)
        part = conv_act_down(n)
        acc = part if acc is None else acc + part

    y = _rms(acc, g3_ref[...])
    for cb in range(d_blocks):
        for s in range(F32_SUBLANES):
            y_ref[cb, pl.ds(s, phase_rows, stride=F32_SUBLANES), :] = (
                y[phase_rows * s:phase_rows * (s + 1), LANES * cb:LANES * (cb + 1)])
    out_ref[...] = x_ref[...] + jnp.concatenate([y_ref[cb] for cb in range(d_blocks)], axis=1)


def conv_ffn(x, g2, w_up, conv_w, conv_b, w_down, g3, *, tm, chunks):
    m, d = x.shape
    f = FFN_HIDDEN
    tiles_per_seq = SEQ // tm
    halo_blocks = tm // HALO
    chunk_blocks = max(b1 - b0 for b0, b1 in chunks)
    once = pl.Buffered(1)
    vec = pl.BlockSpec((1, d), lambda i: (0, 0))
    return pl.pallas_call(
        functools.partial(_ffn_kernel, tm=tm, tiles_per_seq=tiles_per_seq, chunks=chunks),
        grid=(m // tm,),
        in_specs=[
            pl.BlockSpec((tm, d), lambda i: (i, 0)),
            pl.BlockSpec((HALO, d), lambda i: (jnp.maximum(i * halo_blocks - 1, 0), 0)),
            vec,
            pl.BlockSpec((d, f), lambda i: (0, 0), pipeline_mode=once),
            pl.BlockSpec((d, f), lambda i: (0, 1), pipeline_mode=once),
            pl.BlockSpec((CONV_W, f), lambda i: (0, 0)),
            pl.BlockSpec((CONV_W, f), lambda i: (0, 1)),
            pl.BlockSpec((1, f), lambda i: (0, 0)),
            pl.BlockSpec((1, f), lambda i: (0, 1)),
            pl.BlockSpec((f, d), lambda i: (0, 0), pipeline_mode=once),
            vec,
        ],
        out_specs=pl.BlockSpec((tm, d), lambda i: (i, 0)),
        out_shape=jax.ShapeDtypeStruct((m, d), F32),
        scratch_shapes=[
            pltpu.VMEM((HALO + tm, d), BF16),
            pltpu.VMEM((d // LANES, tm, LANES), F32),
        ] + [pltpu.VMEM((chunk_blocks, HALO + tm, LANES), F32)] * 4,
        compiler_params=_cparams(("parallel",)),
        name="conv_ffn",
    )(x, x, g2, w_up, w_up, conv_w, conv_w, conv_b, conv_b, w_down, g3)


def _kv_pair_columns(pair):
    k_base = Q_W + (2 * pair) * KV_W
    v_base = Q_W + (2 * pair + 1) * KV_W
    order = []
    for g in range(KV_GROUPS):
        order += list(range(k_base + HEAD_DIM * g, k_base + HEAD_DIM * (g + 1)))
        order += list(range(v_base + HEAD_DIM * g, v_base + HEAD_DIM * (g + 1)))
    return order


def _nsa_in_weights(w_in):
    main = list(range(Q_W)) + _kv_pair_columns(1) + _kv_pair_columns(2)
    order = np.asarray(main + _kv_pair_columns(0), dtype=np.int32)
    w_gl = w_in[:, Q_W + 6 * KV_W:].reshape(D_MODEL, KV_GROUPS, HPG * N_BRANCH)
    w_gl = jnp.pad(w_gl, ((0, 0), (0, 0), (0, LANES - HPG * N_BRANCH))).reshape(D_MODEL, GATES_W)
    return jnp.concatenate([jnp.take(w_in, order, axis=1), w_gl], axis=1).astype(BF16)


def _cmp_to_slc_t():
    n_cmp = (SEQ - CMP_LEN) // CMP_STRIDE + 1
    cs = np.arange(n_cmp)[:, None] * CMP_STRIDE
    ss = np.arange(N_SLC)[None, :] * SLC_LEN
    ov = np.clip(np.minimum(cs + CMP_LEN, ss + SLC_LEN) - np.maximum(cs, ss), 0, None)
    w = ov.astype(np.float32) / np.float32(CMP_LEN)
    wt = np.zeros((N_SLC, N_CMP_PAD), dtype=np.float32)
    wt[:, :n_cmp] = w.T
    return jnp.asarray(wt, dtype=BF16)


def _block_diag2(a, b):
    za = jnp.zeros((a.shape[0], b.shape[1]), a.dtype)
    zb = jnp.zeros((b.shape[0], a.shape[1]), a.dtype)
    return jnp.concatenate([jnp.concatenate([a, za], axis=1), jnp.concatenate([zb, b], axis=1)], axis=0)


def _nsa_layer(x2, batch, g, w_in, cmp_pe, cmp_w1, cmp_w2, w_out):
    w1 = cmp_w1.reshape(2, CMP_LEN, HEAD_DIM, HEAD_DIM)
    w1bd = jnp.stack([_block_diag2(w1[0, l], w1[1, l]) for l in range(CMP_LEN)])
    wa = w1bd[:CMP_STRIDE].reshape(CMP_STRIDE * LANES, LANES).astype(BF16)
    wb = w1bd[CMP_STRIDE:].reshape(CMP_STRIDE * LANES, LANES).astype(BF16)
    w2bd = _block_diag2(cmp_w2[0], cmp_w2[1]).astype(BF16)
    pe_cat = jnp.concatenate([cmp_pe[0], cmp_pe[1]], axis=1)

    main, cmp, gates = nsa_in_proj(x2, g[0][None], _nsa_in_weights(w_in), tm=512)
    kvc = nsa_compress(cmp.reshape(batch, SEQ, CMP_W), pe_cat, wa, wb, w2bd)
    o = nsa_attention(main.reshape(batch, SEQ, MAIN_W), kvc, gates, _cmp_to_slc_t())
    return proj_norm_res(x2, o.reshape(-1, Q_W), w_out.astype(BF16), g[1][None], tm=512)


def _sg_layer(x2, g, w_in, ln_g, ln_b, w_sp, b_sp, w_out):
    z = norm_matmul_gelu(x2, g[0][None], w_in.astype(BF16), tm=256, chunk=512)
    return sg_out(x2, z, ln_g[None], ln_b[None], w_sp, b_sp.T, w_out.astype(BF16), g[1][None], tm=256)


def kernel(x, norm_gains, nsa_w_in, nsa_cmp_pe, nsa_cmp_w1, nsa_cmp_w2, nsa_w_out, sg_w_in, sg_ln_g,
           sg_ln_b, sg_w_sp, sg_b_sp, sg_w_out, ffn_w_up, ffn_conv_w, ffn_conv_b, ffn_w_down):
    batch, seq, d = x.shape
    assert (seq, d) == (SEQ, D_MODEL)
    x2 = x.reshape(batch * seq, d)
    for i in range(DEPTH):
        g = norm_gains[i]
        slot = i // N_MIXERS
        if i % N_MIXERS == 0:
            x2 = _nsa_layer(x2, batch, g, nsa_w_in[slot], nsa_cmp_pe[slot], nsa_cmp_w1[slot],
                            nsa_cmp_w2[slot], nsa_w_out[slot])
        else:
            x2 = _sg_layer(x2, g, sg_w_in[slot], sg_ln_g[slot], sg_ln_b[slot], sg_w_sp[slot],
                           sg_b_sp[slot], sg_w_out[slot])
        x2 = conv_ffn(x2, g[2][None], ffn_w_up[i].astype(BF16), ffn_conv_w[i], ffn_conv_b[i][None],
                      ffn_w_down[i].astype(BF16), g[3][None], tm=512, chunks=FFN_CHUNKS)
    return x2.reshape(batch, seq, d)
```

```python
import functools

import numpy as np
import jax
import jax.numpy as jnp
from jax import lax
from jax.experimental import pallas as pl
from jax.experimental.pallas import tpu as pltpu

D_MODEL = 1024
SEQ = 2048
DEPTH = 4
N_MIXERS = 2
HEADS = 16
KV_GROUPS = 4
HEAD_DIM = 64
HPG = HEADS // KV_GROUPS
KV_W = KV_GROUPS * HEAD_DIM
N_BRANCH = 3
CMP_LEN = 32
CMP_STRIDE = 16
SLC_LEN = 64
SLC_TOPN = 8
WINDOW = 512
FORCE_SCORE = 1.0e4
NEG_BIG = -1.0e30
SG_WIDTH = 2 * D_MODEL
SG_GROUPS = 16
SG_GROUP_W = SG_WIDTH // SG_GROUPS
SG_CHUNK = 128
FFN_HIDDEN = 2816
CONV_W = 3
EPS = 1e-6

N_SLC = SEQ // SLC_LEN
N_CMP_PAD = SEQ // CMP_STRIDE
Q_W = HEADS * HEAD_DIM
GATE_W = N_BRANCH * HEADS

LANES = 128
F32_SUBLANES = 8
BF16_SUBLANES = 16
VMEM_LIMIT = 48 * 1024 * 1024

KV_PAIR_W = KV_GROUPS * LANES
MAIN_W = Q_W + 2 * KV_PAIR_W
CMP_W = KV_PAIR_W
GATES_W = KV_GROUPS * LANES
ATT_TILE = 128
SLC_UNROLL = 4
ATT_GROUPS = 2

BF16 = jnp.bfloat16
F32 = jnp.float32


def _cparams(sem):
    return pltpu.CompilerParams(dimension_semantics=sem, vmem_limit_bytes=VMEM_LIMIT)


def _rms(x, g):
    return x * lax.rsqrt(jnp.mean(x * x, axis=-1, keepdims=True) + EPS) * g


def _dot(a, b):
    return jnp.dot(a, b, preferred_element_type=F32)


def _norm_matmul_gelu_kernel(x_ref, g_ref, w_ref, o_ref, *, chunk):
    xn = _rms(x_ref[...], g_ref[...]).astype(BF16)
    for c in range(w_ref.shape[1] // chunk):
        o_ref[:, chunk * c:chunk * (c + 1)] = jax.nn.gelu(_dot(xn, w_ref[:, chunk * c:chunk * (c + 1)]))


def norm_matmul_gelu(x, g, w, *, tm, chunk):
    m, k = x.shape
    n = w.shape[1]
    return pl.pallas_call(
        functools.partial(_norm_matmul_gelu_kernel, chunk=chunk),
        grid=(m // tm,),
        in_specs=[
            pl.BlockSpec((tm, k), lambda i: (i, 0)),
            pl.BlockSpec((1, k), lambda i: (0, 0)),
            pl.BlockSpec((k, n), lambda i: (0, 0)),
        ],
        out_specs=pl.BlockSpec((tm, n), lambda i: (i, 0)),
        out_shape=jax.ShapeDtypeStruct((m, n), F32),
        compiler_params=_cparams(("parallel",)),
        name="norm_matmul_gelu",
    )(x, g, w)


def _nsa_in_kernel(x_ref, g_ref, w_ref, main_ref, cmp_ref, gate_ref):
    xn = _rms(x_ref[...], g_ref[...]).astype(BF16)
    main_ref[...] = _dot(xn, w_ref[:, :MAIN_W]).astype(BF16)
    cmp_ref[...] = _dot(xn, w_ref[:, MAIN_W:MAIN_W + CMP_W])
    gate_ref[...] = _dot(xn, w_ref[:, MAIN_W + CMP_W:])


def nsa_in_proj(x, g, w_all, *, tm):
    m, d = x.shape
    n = w_all.shape[1]
    return pl.pallas_call(
        _nsa_in_kernel,
        grid=(m // tm,),
        in_specs=[
            pl.BlockSpec((tm, d), lambda i: (i, 0)),
            pl.BlockSpec((1, d), lambda i: (0, 0)),
            pl.BlockSpec((d, n), lambda i: (0, 0)),
        ],
        out_specs=[
            pl.BlockSpec((tm, MAIN_W), lambda i: (i, 0)),
            pl.BlockSpec((tm, CMP_W), lambda i: (i, 0)),
            pl.BlockSpec((tm, GATES_W), lambda i: (i, 0)),
        ],
        out_shape=[
            jax.ShapeDtypeStruct((m, MAIN_W), BF16),
            jax.ShapeDtypeStruct((m, CMP_W), F32),
            jax.ShapeDtypeStruct((m, GATES_W), F32),
        ],
        compiler_params=_cparams(("parallel",)),
        name="nsa_in_proj",
    )(x, g, w_all)


def _compress_kernel(x0_ref, x1_ref, x2_ref, x3_ref, pe_ref, wa_ref, wb_ref, w2_ref, kv_ref,
                     ca_ref, cb_ref):
    nck = N_CMP_PAD
    for l in range(CMP_STRIDE):
        for g, x_ref in enumerate((x0_ref, x1_ref, x2_ref, x3_ref)):
            xg = x_ref[pl.ds(l, nck, stride=CMP_STRIDE), :]
            ca_ref[nck * g:nck * (g + 1), LANES * l:LANES * (l + 1)] = (
                xg + pe_ref[l:l + 1, :]).astype(BF16)
            cb_ref[nck * g:nck * (g + 1), LANES * l:LANES * (l + 1)] = (
                xg + pe_ref[CMP_STRIDE + l:CMP_STRIDE + l + 1, :]).astype(BF16)
    a = _dot(ca_ref[...], wa_ref[...])
    bm = _dot(cb_ref[...], wb_ref[...])
    for g in range(KV_GROUPS):
        ag = a[nck * g:nck * (g + 1)]
        bg = bm[nck * g:nck * (g + 1)]
        pre = ag + pltpu.roll(bg, nck - 1, 0)
        h = jax.nn.gelu(pre).astype(BF16)
        kv_ref[nck * g:nck * (g + 1), :] = _dot(h, w2_ref[...]).astype(BF16)


def nsa_compress(cmp3, pe_cat, wa, wb, w2bd):
    b = cmp3.shape[0]
    rows = KV_GROUPS * N_CMP_PAD
    feat = CMP_STRIDE * LANES
    x_specs = [pl.BlockSpec((None, SEQ, LANES), functools.partial(lambda i, g: (i, 0, g), g=g))
               for g in range(KV_GROUPS)]
    return pl.pallas_call(
        _compress_kernel,
        grid=(b,),
        in_specs=x_specs + [
            pl.BlockSpec((CMP_LEN, LANES), lambda i: (0, 0)),
            pl.BlockSpec((feat, LANES), lambda i: (0, 0)),
            pl.BlockSpec((feat, LANES), lambda i: (0, 0)),
            pl.BlockSpec((LANES, LANES), lambda i: (0, 0)),
        ],
        out_specs=pl.BlockSpec((None, rows, LANES), lambda i: (i, 0, 0)),
        out_shape=jax.ShapeDtypeStruct((b, rows, LANES), BF16),
        scratch_shapes=[pltpu.VMEM((rows, feat), BF16), pltpu.VMEM((rows, feat), BF16)],
        compiler_params=_cparams(("parallel",)),
        name="nsa_compress",
    )(cmp3, cmp3, cmp3, cmp3, pe_cat, wa, wb, w2bd)


def _mask_rows(s, mask, fill, tq):
    return jnp.concatenate(
        [jnp.where(mask, s[:, tq * h:tq * (h + 1)], fill) for h in range(HPG)], axis=1)


def _scores(tiles, q4t, tq):
    out = []
    for lhs, mask in tiles:
        s = _dot(lhs, q4t)
        out.append(s if mask is None else _mask_rows(s, mask, NEG_BIG, tq))
    return out


def _softmax_pv(carry, ss, vts):
    m_new = None if carry is None else carry[0]
    for s in ss:
        mx = jnp.max(s, axis=0, keepdims=True)
        m_new = mx if m_new is None else jnp.maximum(m_new, mx)
    l = acc = None
    if carry is not None:
        alpha = jnp.exp(carry[0] - m_new)
        l, acc = alpha * carry[1], alpha * carry[2]
    for s, vt in zip(ss, vts):
        p = jnp.exp(s - m_new)
        ps = jnp.sum(p, axis=0, keepdims=True)
        pv = _dot(vt, p.astype(BF16))
        l = ps if l is None else l + ps
        acc = pv if acc is None else acc + pv
    return m_new, l, acc


def _nsa_attn_kernel(*refs):
    ng = ATT_GROUPS
    q_ref = refs[0]
    kvs_refs = refs[1:1 + ng]
    kvw_refs = refs[1 + ng:1 + 2 * ng]
    kvc_ref, gl_ref, wt_ref, o_ref, kse_ref, vst_ref, vwt_ref, vct_ref, s0_ref, s1_ref = refs[1 + 2 * ng:]
    tq = tk = ATT_TILE
    unroll = SLC_UNROLL
    qw = HPG * HEAD_DIM
    lanes4 = HPG * tq
    win_tiles = WINDOW // tk
    qi = pl.program_id(2)
    t0 = qi * tq
    groups = range(ng)

    @pl.when(qi == 0)
    def _():
        lane = lax.broadcasted_iota(jnp.int32, (tk, LANES), 1)
        krow = lax.broadcasted_iota(jnp.int32, (tk, LANES), 0)
        for gg in groups:
            for c in range(SEQ // tk):
                kv = kvs_refs[gg][tk * c:tk * (c + 1), :].astype(F32)
                vst_ref[gg, c] = kv.T[HEAD_DIM:, :].astype(BF16)
                vwt_ref[gg, c] = kvw_refs[gg][tk * c:tk * (c + 1), :].astype(F32).T[HEAD_DIM:, :].astype(BF16)
                onehot = jnp.where(lane - HEAD_DIM == ((tk * c + krow) >> 6), 1.0, 0.0)
                kse_ref[gg, tk * c:tk * (c + 1), :] = jnp.where(lane < HEAD_DIM, kv, onehot).astype(BF16)
            vct_ref[gg] = kvc_ref[N_CMP_PAD * gg:N_CMP_PAD * (gg + 1), :].astype(F32).T[HEAD_DIM:, :].astype(BF16)

    kr = lax.broadcasted_iota(jnp.int32, (tk, tq), 0)
    tc = lax.broadcasted_iota(jnp.int32, (tk, tq), 1)
    diag = kr <= tc
    zpad = jnp.zeros((HEAD_DIM, tq), F32)

    qts = [q_ref[:, qw * gg:qw * (gg + 1)].astype(F32).T * (HEAD_DIM ** -0.5) for gg in groups]

    def stack_q(qt, extra):
        return jnp.concatenate(
            [jnp.concatenate([qt[HEAD_DIM * h:HEAD_DIM * (h + 1)], extra], axis=0) for h in range(HPG)],
            axis=1).astype(BF16)

    q4t = [stack_q(qts[gg], zpad) for gg in groups]

    win_s, win_vt, cmp_s = [], [], []
    for gg in groups:
        tiles, vts = [], []
        for u in range(win_tiles + 1):
            kt = qi - win_tiles + u
            exists = kt >= 0
            ktc = jnp.maximum(kt, 0)
            k0 = pl.multiple_of(ktc * tk, tk)
            if u == 0:
                mask = jnp.logical_and(kr > tc, exists)
            elif u < win_tiles:
                mask = jnp.logical_and(kr >= 0, exists)
            else:
                mask = diag
            tiles.append((kvw_refs[gg][pl.ds(k0, tk), :], mask))
            vts.append(vwt_ref[gg, ktc])
        win_s.append(_scores(tiles, q4t[gg], tq))
        win_vt.append(vts)
        cmp_s.append(_dot(kvc_ref[N_CMP_PAD * gg:N_CMP_PAD * (gg + 1), :], q4t[gg]))

    cmask = (kr * CMP_STRIDE + (CMP_LEN - 1)) <= (t0 + tc)
    o_cmp, imp4 = [], []
    for gg in groups:
        s = _mask_rows(cmp_s[gg], cmask, NEG_BIG, tq)
        m = jnp.max(s, axis=0, keepdims=True)
        e = _mask_rows(jnp.exp(s - m), cmask, 0.0, tq)
        den = jnp.sum(e, axis=0, keepdims=True)
        p = (e * (1.0 / jnp.maximum(den, 1e-30))).astype(BF16)
        o_cmp.append(_dot(vct_ref[gg], p))
        imp4.append(_dot(wt_ref[...], p))

    o_win = []
    for gg in groups:
        _, l_w, acc_w = _softmax_pv(None, win_s[gg], win_vt[gg])
        o_win.append(acc_w * (1.0 / l_w))

    j = lax.broadcasted_iota(jnp.int32, (N_SLC, tq), 0)
    blk = (t0 + lax.broadcasted_iota(jnp.int32, (N_SLC, tq), 1)) >> 6
    valid = j <= blk
    forced = (j == 0) | (j == blk) | (j == blk - 1)
    q4t_sel = []
    for gg in groups:
        imp = imp4[gg][:, 0:tq]
        for h in range(1, HPG):
            imp = imp + imp4[gg][:, tq * h:tq * (h + 1)]
        score = jnp.where(valid, imp + jnp.where(forced, FORCE_SCORE, 0.0), -FORCE_SCORE)
        rank = jnp.zeros(imp.shape, F32)
        for i in range(N_SLC):
            ri = score[i:i + 1, :]
            beats = jnp.where(ri > score, 1.0, jnp.where(ri == score, jnp.where(j > i, 1.0, 0.0), 0.0))
            rank = rank + beats
        sel_bias = jnp.where(valid, jnp.where(rank < float(SLC_TOPN), 0.0, NEG_BIG), NEG_BIG)
        q4t_sel.append(stack_q(qts[gg], jnp.concatenate([sel_bias, jnp.zeros((HEAD_DIM - N_SLC, tq), F32)], axis=0)))

    s_refs = (s0_ref, s1_ref)
    n_main = qi // unroll
    tail_base = n_main * unroll
    for gg in groups:
        for u in range(unroll):
            k0 = pl.multiple_of((tail_base + u) * tk, tk)
            s = _dot(kse_ref[gg, pl.ds(k0, tk), :], q4t_sel[gg])
            s0_ref[gg, u] = _mask_rows(s, (k0 + kr) <= (t0 + tc), NEG_BIG, tq)

    def process(parity):
        def run(args):
            i, carries = args
            base = jnp.where(i == 0, tail_base, (i - 1) * unroll)
            return tuple(
                _softmax_pv(carries[gg], [s_refs[parity][gg, u] for u in range(unroll)],
                            [vst_ref[gg, base + u] for u in range(unroll)]) for gg in groups)
        return run

    def step(parity):
        def run(args):
            i, carries = args
            for gg in groups:
                for u in range(unroll):
                    k0 = pl.multiple_of((i * unroll + u) * tk, tk)
                    s_refs[1 - parity][gg, u] = _dot(kse_ref[gg, pl.ds(k0, tk), :], q4t_sel[gg])
            return process(parity)(args)
        return run

    init = (jnp.full((1, lanes4), NEG_BIG, F32), jnp.zeros((1, lanes4), F32),
            jnp.zeros((HEAD_DIM, lanes4), F32))
    carries = lax.fori_loop(
        0, n_main, lambda i, c: lax.cond(i % 2 == 0, step(0), step(1), (i, c)), (init,) * ng)
    carries = lax.cond(n_main % 2 == 0, process(0), process(1), (n_main, carries))

    for gg in groups:
        _, l_s, acc_s = carries[gg]
        o_slc = acc_s * (1.0 / l_s)
        sg = jax.nn.sigmoid(gl_ref[:, LANES * gg:LANES * (gg + 1)].T[:BF16_SUBLANES, :])
        heads = []
        for h in range(HPG):
            sl = slice(tq * h, tq * (h + 1))
            r = N_BRANCH * h
            heads.append(sg[r:r + 1] * o_cmp[gg][:, sl] + sg[r + 1:r + 2] * o_slc[:, sl]
                         + sg[r + 2:r + 3] * o_win[gg][:, sl])
        o_ref[:, qw * gg:qw * (gg + 1)] = jnp.concatenate(heads, axis=0).T.astype(BF16)


def nsa_attention(main3, kvc, gates, wcs_t):
    b = main3.shape[0]
    ng = ATT_GROUPS
    tq = ATT_TILE
    nq = SEQ // tq
    qw = ng * HPG * HEAD_DIM
    kvs_base = Q_W // LANES
    kvw_base = kvs_base + KV_GROUPS

    def kv_spec(base, gg):
        return pl.BlockSpec((None, SEQ, LANES), lambda b_, gp, i: (b_, 0, base + gp * ng + gg))

    return pl.pallas_call(
        _nsa_attn_kernel,
        grid=(b, KV_GROUPS // ng, nq),
        in_specs=[pl.BlockSpec((None, tq, qw), lambda b_, gp, i: (b_, i, gp))]
        + [kv_spec(kvs_base, gg) for gg in range(ng)]
        + [kv_spec(kvw_base, gg) for gg in range(ng)]
        + [
            pl.BlockSpec((None, ng * N_CMP_PAD, LANES), lambda b_, gp, i: (b_, gp, 0)),
            pl.BlockSpec((tq, ng * LANES), lambda b_, gp, i: (b_ * nq + i, gp)),
            pl.BlockSpec((N_SLC, N_CMP_PAD), lambda b_, gp, i: (0, 0)),
        ],
        out_specs=pl.BlockSpec((None, tq, qw), lambda b_, gp, i: (b_, i, gp)),
        out_shape=jax.ShapeDtypeStruct((b, SEQ, Q_W), BF16),
        scratch_shapes=[
            pltpu.VMEM((ng, SEQ, LANES), BF16),
            pltpu.VMEM((ng, SEQ // tq, HEAD_DIM, tq), BF16),
            pltpu.VMEM((ng, SEQ // tq, HEAD_DIM, tq), BF16),
            pltpu.VMEM((ng, HEAD_DIM, N_CMP_PAD), BF16),
            pltpu.VMEM((ng, SLC_UNROLL, tq, HPG * tq), F32),
            pltpu.VMEM((ng, SLC_UNROLL, tq, HPG * tq), F32),
        ],
        compiler_params=_cparams(("parallel", "parallel", "arbitrary")),
        name="nsa_attention",
    )(main3, *([main3] * (2 * ng)), kvc, gates, wcs_t)


def _proj_norm_res_kernel(x_ref, a_ref, w_ref, g_ref, out_ref):
    out_ref[...] = x_ref[...] + _rms(_dot(a_ref[...], w_ref[...]), g_ref[...])


def proj_norm_res(x, a, w, g, *, tm):
    m, d = x.shape
    k = a.shape[1]
    row = pl.BlockSpec((tm, d), lambda i: (i, 0))
    return pl.pallas_call(
        _proj_norm_res_kernel,
        grid=(m // tm,),
        in_specs=[row, pl.BlockSpec((tm, k), lambda i: (i, 0)), pl.BlockSpec((k, d), lambda i: (0, 0)),
                  pl.BlockSpec((1, d), lambda i: (0, 0))],
        out_specs=row,
        out_shape=jax.ShapeDtypeStruct((m, d), F32),
        compiler_params=_cparams(("parallel",)),
        name="proj_norm_res",
    )(x, a, w, g)


def _sg_out_kernel(x_ref, u_ref, v_ref, lng_ref, lnb_ref, wsp_ref, bt_ref, wo_ref, g1_ref,
                   out_ref, wm_ref, gated_ref, *, tm):
    @pl.when(pl.program_id(0) == 0)
    def _():
        t = lax.broadcasted_iota(jnp.int32, (SG_CHUNK, SG_CHUNK), 0)
        s = lax.broadcasted_iota(jnp.int32, (SG_CHUNK, SG_CHUNK), 1)
        for g in range(SG_GROUPS):
            wm_ref[g] = jnp.where(s <= t, wsp_ref[g], 0.0).astype(BF16)

    v = v_ref[...]
    mu = jnp.mean(v, axis=-1, keepdims=True)
    vc = v - mu
    var = jnp.mean(vc * vc, axis=-1, keepdims=True)
    vn = (vc * lax.rsqrt(var + EPS) * lng_ref[...] + lnb_ref[...]).astype(BF16)
    for c in range(tm // SG_CHUNK):
        r0, r1 = SG_CHUNK * c, SG_CHUNK * (c + 1)
        for g in range(SG_GROUPS):
            c0, c1 = SG_GROUP_W * g, SG_GROUP_W * (g + 1)
            mixed = _dot(wm_ref[g], vn[r0:r1, c0:c1]) + bt_ref[:, g:g + 1]
            gated_ref[r0:r1, c0:c1] = (u_ref[r0:r1, c0:c1] * mixed).astype(BF16)
    m = _dot(gated_ref[...], wo_ref[...])
    out_ref[...] = x_ref[...] + _rms(m, g1_ref[...])


def sg_out(x, z, ln_g, ln_b, w_sp, b_sp_t, w_out, g1, *, tm):
    m, d = x.shape
    e = SG_WIDTH
    row = pl.BlockSpec((tm, d), lambda i: (i, 0))
    return pl.pallas_call(
        functools.partial(_sg_out_kernel, tm=tm),
        grid=(m // tm,),
        in_specs=[
            row,
            pl.BlockSpec((tm, e), lambda i: (i, 0)),
            pl.BlockSpec((tm, e), lambda i: (i, 1)),
            pl.BlockSpec((1, e), lambda i: (0, 0)),
            pl.BlockSpec((1, e), lambda i: (0, 0)),
            pl.BlockSpec((SG_GROUPS, SG_CHUNK, SG_CHUNK), lambda i: (0, 0, 0)),
            pl.BlockSpec((SG_CHUNK, SG_GROUPS), lambda i: (0, 0)),
            pl.BlockSpec((e, d), lambda i: (0, 0)),
            pl.BlockSpec((1, d), lambda i: (0, 0)),
        ],
        out_specs=row,
        out_shape=jax.ShapeDtypeStruct((m, d), F32),
        scratch_shapes=[pltpu.VMEM((SG_GROUPS, SG_CHUNK, SG_CHUNK), BF16), pltpu.VMEM((tm, e), BF16)],
        compiler_params=_cparams(("arbitrary",)),
        name="sg_out",
    )(x, z, z, ln_g, ln_b, w_sp, b_sp_t, w_out, g1)


HALO = BF16_SUBLANES
FFN_CHUNKS = tuple((b, min(b + 4, FFN_HIDDEN // LANES)) for b in range(0, FFN_HIDDEN // LANES, 4))


def _ffn_kernel(x_ref, xh_ref, g2_ref, wg_ref, wv_ref, cwg_ref, cwv_ref, cbg_ref, cbv_ref,
                wd_ref, g3_ref, out_ref, xn_ref, y_ref, hg0_ref, hg1_ref, hv0_ref, hv1_ref,
                *, tm, tiles_per_seq, chunks):
    i = pl.program_id(0)
    phase_rows = tm // F32_SUBLANES
    d_blocks = D_MODEL // LANES

    xn_ref[HALO:, :] = _rms(x_ref[...], g2_ref[...]).astype(BF16)
    hn = _rms(xh_ref[...], g2_ref[...])
    seq_start = (i % tiles_per_seq) == 0
    xn_ref[:HALO, :] = jnp.where(seq_start, 0.0, hn).astype(BF16)

    hg_refs, hv_refs = (hg0_ref, hg1_ref), (hv0_ref, hv1_ref)

    def phases(h_ref, slot):
        return [h_ref[slot, pl.ds(HALO + p, phase_rows, stride=F32_SUBLANES), :]
                for p in range(1 - CONV_W, F32_SUBLANES)]

    def conv(ph, cw_ref, cb_ref, b, s):
        lo, hi = LANES * b, LANES * (b + 1)
        out = cb_ref[:, lo:hi]
        for k in range(CONV_W):
            out = out + cw_ref[k:k + 1, lo:hi] * ph[s + k]
        return out

    xa = xn_ref[...]

    def up_proj(n):
        b0, b1 = chunks[n]
        c0, c1 = LANES * b0, LANES * b1
        hg = _dot(xa, wg_ref[:, c0:c1])
        hv = _dot(xa, wv_ref[:, c0:c1])
        for b in range(b0, b1):
            hg_refs[n % 2][b - b0] = hg[:, LANES * (b - b0):LANES * (b - b0 + 1)]
            hv_refs[n % 2][b - b0] = hv[:, LANES * (b - b0):LANES * (b - b0 + 1)]

    def conv_act_down(n):
        b0, b1 = chunks[n]
        cols = []
        for b in range(b0, b1):
            pg = phases(hg_refs[n % 2], b - b0)
            pv = phases(hv_refs[n % 2], b - b0)
            rows = []
            for s in range(F32_SUBLANES):
                cg = conv(pg, cwg_ref, cbg_ref, b, s)
                cv = conv(pv, cwv_ref, cbv_ref, b, s)
                rows.append((jax.nn.silu(cg) * cv).astype(BF16))
            cols.append(jnp.concatenate(rows, axis=0))
        act = jnp.concatenate(cols, axis=1)
        return _dot(act, wd_ref[LANES * b0:LANES * b1, :])

    acc = None
    up_proj(0)
    for n in range(len(chunks)):
        if n + 1 < len(chunks):
            up_proj(n + 1)
        part = conv_act_down(n)
        acc = part if acc is None else acc + part

    y = _rms(acc, g3_ref[...])
    for cb in range(d_blocks):
        for s in range(F32_SUBLANES):
            y_ref[cb, pl.ds(s, phase_rows, stride=F32_SUBLANES), :] = (
                y[phase_rows * s:phase_rows * (s + 1), LANES * cb:LANES * (cb + 1)])
    out_ref[...] = x_ref[...] + jnp.concatenate([y_ref[cb] for cb in range(d_blocks)], axis=1)


def conv_ffn(x, g2, w_up, conv_w, conv_b, w_down, g3, *, tm, chunks):
    m, d = x.shape
    f = FFN_HIDDEN
    tiles_per_seq = SEQ // tm
    halo_blocks = tm // HALO
    chunk_blocks = max(b1 - b0 for b0, b1 in chunks)
    once = pl.Buffered(1)
    vec = pl.BlockSpec((1, d), lambda i: (0, 0))
    return pl.pallas_call(
        functools.partial(_ffn_kernel, tm=tm, tiles_per_seq=tiles_per_seq, chunks=chunks),
        grid=(m // tm,),
        in_specs=[
            pl.BlockSpec((tm, d), lambda i: (i, 0)),
            pl.BlockSpec((HALO, d), lambda i: (jnp.maximum(i * halo_blocks - 1, 0), 0)),
            vec,
            pl.BlockSpec((d, f), lambda i: (0, 0), pipeline_mode=once),
            pl.BlockSpec((d, f), lambda i: (0, 1), pipeline_mode=once),
            pl.BlockSpec((CONV_W, f), lambda i: (0, 0)),
            pl.BlockSpec((CONV_W, f), lambda i: (0, 1)),
            pl.BlockSpec((1, f), lambda i: (0, 0)),
            pl.BlockSpec((1, f), lambda i: (0, 1)),
            pl.BlockSpec((f, d), lambda i: (0, 0), pipeline_mode=once),
            vec,
        ],
        out_specs=pl.BlockSpec((tm, d), lambda i: (i, 0)),
        out_shape=jax.ShapeDtypeStruct((m, d), F32),
        scratch_shapes=[
            pltpu.VMEM((HALO + tm, d), BF16),
            pltpu.VMEM((d // LANES, tm, LANES), F32),
        ] + [pltpu.VMEM((chunk_blocks, HALO + tm, LANES), F32)] * 4,
        compiler_params=_cparams(("parallel",)),
        name="conv_ffn",
    )(x, x, g2, w_up, w_up, conv_w, conv_w, conv_b, conv_b, w_down, g3)


def _kv_pair_columns(pair):
    k_base = Q_W + (2 * pair) * KV_W
    v_base = Q_W + (2 * pair + 1) * KV_W
    order = []
    for g in range(KV_GROUPS):
        order += list(range(k_base + HEAD_DIM * g, k_base + HEAD_DIM * (g + 1)))
        order += list(range(v_base + HEAD_DIM * g, v_base + HEAD_DIM * (g + 1)))
    return order


def _nsa_in_weights(w_in):
    main = list(range(Q_W)) + _kv_pair_columns(1) + _kv_pair_columns(2)
    order = np.asarray(main + _kv_pair_columns(0), dtype=np.int32)
    w_gl = w_in[:, Q_W + 6 * KV_W:].reshape(D_MODEL, KV_GROUPS, HPG * N_BRANCH)
    w_gl = jnp.pad(w_gl, ((0, 0), (0, 0), (0, LANES - HPG * N_BRANCH))).reshape(D_MODEL, GATES_W)
    return jnp.concatenate([jnp.take(w_in, order, axis=1), w_gl], axis=1).astype(BF16)


def _cmp_to_slc_t():
    n_cmp = (SEQ - CMP_LEN) // CMP_STRIDE + 1
    cs = np.arange(n_cmp)[:, None] * CMP_STRIDE
    ss = np.arange(N_SLC)[None, :] * SLC_LEN
    ov = np.clip(np.minimum(cs + CMP_LEN, ss + SLC_LEN) - np.maximum(cs, ss), 0, None)
    w = ov.astype(np.float32) / np.float32(CMP_LEN)
    wt = np.zeros((N_SLC, N_CMP_PAD), dtype=np.float32)
    wt[:, :n_cmp] = w.T
    return jnp.asarray(wt, dtype=BF16)


def _block_diag2(a, b):
    za = jnp.zeros((a.shape[0], b.shape[1]), a.dtype)
    zb = jnp.zeros((b.shape[0], a.shape[1]), a.dtype)
    return jnp.concatenate([jnp.concatenate([a, za], axis=1), jnp.concatenate([zb, b], axis=1)], axis=0)


def _nsa_layer(x2, batch, g, w_in, cmp_pe, cmp_w1, cmp_w2, w_out):
    w1 = cmp_w1.reshape(2, CMP_LEN, HEAD_DIM, HEAD_DIM)
    w1bd = jnp.stack([_block_diag2(w1[0, l], w1[1, l]) for l in range(CMP_LEN)])
    wa = w1bd[:CMP_STRIDE].reshape(CMP_STRIDE * LANES, LANES).astype(BF16)
    wb = w1bd[CMP_STRIDE:].reshape(CMP_STRIDE * LANES, LANES).astype(BF16)
    w2bd = _block_diag2(cmp_w2[0], cmp_w2[1]).astype(BF16)
    pe_cat = jnp.concatenate([cmp_pe[0], cmp_pe[1]], axis=1)

    main, cmp, gates = nsa_in_proj(x2, g[0][None], _nsa_in_weights(w_in), tm=512)
    kvc = nsa_compress(cmp.reshape(batch, SEQ, CMP_W), pe_cat, wa, wb, w2bd)
    o = nsa_attention(main.reshape(batch, SEQ, MAIN_W), kvc, gates, _cmp_to_slc_t())
    return proj_norm_res(x2, o.reshape(-1, Q_W), w_out.astype(BF16), g[1][None], tm=512)


def _sg_layer(x2, g, w_in, ln_g, ln_b, w_sp, b_sp, w_out):
    z = norm_matmul_gelu(x2, g[0][None], w_in.astype(BF16), tm=256, chunk=512)
    return sg_out(x2, z, ln_g[None], ln_b[None], w_sp, b_sp.T, w_out.astype(BF16), g[1][None], tm=256)


def kernel(x, norm_gains, nsa_w_in, nsa_cmp_pe, nsa_cmp_w1, nsa_cmp_w2, nsa_w_out, sg_w_in, sg_ln_g,
           sg_ln_b, sg_w_sp, sg_b_sp, sg_w_out, ffn_w_up, ffn_conv_w, ffn_conv_b, ffn_w_down):
    batch, seq, d = x.shape
    assert (seq, d) == (SEQ, D_MODEL)
    x2 = x.reshape(batch * seq, d)
    for i in range(DEPTH):
        g = norm_gains[i]
        slot = i // N_MIXERS
        if i % N_MIXERS == 0:
            x2 = _nsa_layer(x2, batch, g, nsa_w_in[slot], nsa_cmp_pe[slot], nsa_cmp_w1[slot],
                            nsa_cmp_w2[slot], nsa_w_out[slot])
        else:
            x2 = _sg_layer(x2, g, sg_w_in[slot], sg_ln_g[slot], sg_ln_b[slot], sg_w_sp[slot],
                           sg_b_sp[slot], sg_w_out[slot])
        x2 = conv_ffn(x2, g[2][None], ffn_w_up[i].astype(BF16), ffn_conv_w[i], ffn_conv_b[i][None],
                      ffn_w_down[i].astype(BF16), g[3][None], tm=512, chunks=FFN_CHUNKS)
    return x2.reshape(batch, seq, d)
```

```python
import functools

import numpy as np
import jax
import jax.numpy as jnp
from jax import lax
from jax.experimental import pallas as pl
from jax.experimental.pallas import tpu as pltpu

D_MODEL = 1024
SEQ = 2048
DEPTH = 4
N_MIXERS = 2
HEADS = 16
KV_GROUPS = 4
HEAD_DIM = 64
HPG = HEADS // KV_GROUPS
KV_W = KV_GROUPS * HEAD_DIM
N_BRANCH = 3
CMP_LEN = 32
CMP_STRIDE = 16
SLC_LEN = 64
SLC_TOPN = 8
WINDOW = 512
FORCE_SCORE = 1.0e4
NEG_BIG = -1.0e30
SG_WIDTH = 2 * D_MODEL
SG_GROUPS = 16
SG_GROUP_W = SG_WIDTH // SG_GROUPS
SG_CHUNK = 128
FFN_HIDDEN = 2816
CONV_W = 3
EPS = 1e-6

N_SLC = SEQ // SLC_LEN
N_CMP_PAD = SEQ // CMP_STRIDE
Q_W = HEADS * HEAD_DIM
GATE_W = N_BRANCH * HEADS

LANES = 128
F32_SUBLANES = 8
BF16_SUBLANES = 16
VMEM_LIMIT = 48 * 1024 * 1024

KV_PAIR_W = KV_GROUPS * LANES
MAIN_W = Q_W + 2 * KV_PAIR_W
CMP_W = KV_PAIR_W
GATES_W = KV_GROUPS * LANES
ATT_TILE = 128
SLC_UNROLL = 4
ATT_GROUPS = 4

BF16 = jnp.bfloat16
F32 = jnp.float32


def _cparams(sem):
    return pltpu.CompilerParams(dimension_semantics=sem, vmem_limit_bytes=VMEM_LIMIT)


def _rms(x, g):
    return x * lax.rsqrt(jnp.mean(x * x, axis=-1, keepdims=True) + EPS) * g


def _dot(a, b):
    return jnp.dot(a, b, preferred_element_type=F32)


def _norm_matmul_gelu_kernel(x_ref, g_ref, w_ref, o_ref, *, chunk):
    xn = _rms(x_ref[...], g_ref[...]).astype(BF16)
    for c in range(w_ref.shape[1] // chunk):
        o_ref[:, chunk * c:chunk * (c + 1)] = jax.nn.gelu(_dot(xn, w_ref[:, chunk * c:chunk * (c + 1)]))


def norm_matmul_gelu(x, g, w, *, tm, chunk):
    m, k = x.shape
    n = w.shape[1]
    return pl.pallas_call(
        functools.partial(_norm_matmul_gelu_kernel, chunk=chunk),
        grid=(m // tm,),
        in_specs=[
            pl.BlockSpec((tm, k), lambda i: (i, 0)),
            pl.BlockSpec((1, k), lambda i: (0, 0)),
            pl.BlockSpec((k, n), lambda i: (0, 0), pipeline_mode=pl.Buffered(1)),
        ],
        out_specs=pl.BlockSpec((tm, n), lambda i: (i, 0)),
        out_shape=jax.ShapeDtypeStruct((m, n), F32),
        compiler_params=_cparams(("parallel",)),
        name="norm_matmul_gelu",
    )(x, g, w)


def _nsa_in_kernel(x_ref, g_ref, w_ref, main_ref, cmp_ref, gate_ref):
    xn = _rms(x_ref[...], g_ref[...]).astype(BF16)
    main_ref[...] = _dot(xn, w_ref[:, :MAIN_W]).astype(BF16)
    cmp_ref[...] = _dot(xn, w_ref[:, MAIN_W:MAIN_W + CMP_W])
    gate_ref[...] = _dot(xn, w_ref[:, MAIN_W + CMP_W:])


def nsa_in_proj(x, g, w_all, *, tm):
    m, d = x.shape
    n = w_all.shape[1]
    return pl.pallas_call(
        _nsa_in_kernel,
        grid=(m // tm,),
        in_specs=[
            pl.BlockSpec((tm, d), lambda i: (i, 0)),
            pl.BlockSpec((1, d), lambda i: (0, 0)),
            pl.BlockSpec((d, n), lambda i: (0, 0)),
        ],
        out_specs=[
            pl.BlockSpec((tm, MAIN_W), lambda i: (i, 0)),
            pl.BlockSpec((tm, CMP_W), lambda i: (i, 0)),
            pl.BlockSpec((tm, GATES_W), lambda i: (i, 0)),
        ],
        out_shape=[
            jax.ShapeDtypeStruct((m, MAIN_W), BF16),
            jax.ShapeDtypeStruct((m, CMP_W), F32),
            jax.ShapeDtypeStruct((m, GATES_W), F32),
        ],
        compiler_params=_cparams(("parallel",)),
        name="nsa_in_proj",
    )(x, g, w_all)


def _compress_kernel(x0_ref, x1_ref, x2_ref, x3_ref, pe_ref, wa_ref, wb_ref, w2_ref, kv_ref,
                     ca_ref, cb_ref):
    nck = N_CMP_PAD
    for l in range(CMP_STRIDE):
        for g, x_ref in enumerate((x0_ref, x1_ref, x2_ref, x3_ref)):
            xg = x_ref[pl.ds(l, nck, stride=CMP_STRIDE), :]
            ca_ref[nck * g:nck * (g + 1), LANES * l:LANES * (l + 1)] = (
                xg + pe_ref[l:l + 1, :]).astype(BF16)
            cb_ref[nck * g:nck * (g + 1), LANES * l:LANES * (l + 1)] = (
                xg + pe_ref[CMP_STRIDE + l:CMP_STRIDE + l + 1, :]).astype(BF16)
    a = _dot(ca_ref[...], wa_ref[...])
    bm = _dot(cb_ref[...], wb_ref[...])
    for g in range(KV_GROUPS):
        ag = a[nck * g:nck * (g + 1)]
        bg = bm[nck * g:nck * (g + 1)]
        pre = ag + pltpu.roll(bg, nck - 1, 0)
        h = jax.nn.gelu(pre).astype(BF16)
        kv_ref[nck * g:nck * (g + 1), :] = _dot(h, w2_ref[...]).astype(BF16)


def nsa_compress(cmp3, pe_cat, wa, wb, w2bd):
    b = cmp3.shape[0]
    rows = KV_GROUPS * N_CMP_PAD
    feat = CMP_STRIDE * LANES
    x_specs = [pl.BlockSpec((None, SEQ, LANES), functools.partial(lambda i, g: (i, 0, g), g=g))
               for g in range(KV_GROUPS)]
    return pl.pallas_call(
        _compress_kernel,
        grid=(b,),
        in_specs=x_specs + [
            pl.BlockSpec((CMP_LEN, LANES), lambda i: (0, 0)),
            pl.BlockSpec((feat, LANES), lambda i: (0, 0)),
            pl.BlockSpec((feat, LANES), lambda i: (0, 0)),
            pl.BlockSpec((LANES, LANES), lambda i: (0, 0)),
        ],
        out_specs=pl.BlockSpec((None, rows, LANES), lambda i: (i, 0, 0)),
        out_shape=jax.ShapeDtypeStruct((b, rows, LANES), BF16),
        scratch_shapes=[pltpu.VMEM((rows, feat), BF16), pltpu.VMEM((rows, feat), BF16)],
        compiler_params=_cparams(("parallel",)),
        name="nsa_compress",
    )(cmp3, cmp3, cmp3, cmp3, pe_cat, wa, wb, w2bd)


def _mask_rows(s, mask, fill, tq):
    return jnp.concatenate(
        [jnp.where(mask, s[:, tq * h:tq * (h + 1)], fill) for h in range(HPG)], axis=1)


def _scores(tiles, q4t, tq):
    out = []
    for lhs, mask in tiles:
        s = _dot(lhs, q4t)
        out.append(s if mask is None else _mask_rows(s, mask, NEG_BIG, tq))
    return out


def _softmax_pv(carry, ss, vts):
    m_new = None if carry is None else carry[0]
    for s in ss:
        mx = jnp.max(s, axis=0, keepdims=True)
        m_new = mx if m_new is None else jnp.maximum(m_new, mx)
    l = acc = None
    if carry is not None:
        alpha = jnp.exp(carry[0] - m_new)
        l, acc = alpha * carry[1], alpha * carry[2]
    for s, vt in zip(ss, vts):
        p = jnp.exp(s - m_new)
        ps = jnp.sum(p, axis=0, keepdims=True)
        pv = _dot(vt, p.astype(BF16))
        l = ps if l is None else l + ps
        acc = pv if acc is None else acc + pv
    return m_new, l, acc


def _nsa_attn_kernel(*refs):
    ng = ATT_GROUPS
    q_ref = refs[0]
    kvs_refs = refs[1:1 + ng]
    kvw_refs = refs[1 + ng:1 + 2 * ng]
    kvc_ref, gl_ref, wt_ref, o_ref, kse_ref, vst_ref, vwt_ref, vct_ref, s0_ref, s1_ref = refs[1 + 2 * ng:]
    tq = tk = ATT_TILE
    unroll = SLC_UNROLL
    qw = HPG * HEAD_DIM
    lanes4 = HPG * tq
    win_tiles = WINDOW // tk
    qi = pl.program_id(2)
    t0 = qi * tq
    groups = range(ng)

    @pl.when(qi == 0)
    def _():
        lane = lax.broadcasted_iota(jnp.int32, (tk, LANES), 1)
        krow = lax.broadcasted_iota(jnp.int32, (tk, LANES), 0)
        for gg in groups:
            for c in range(SEQ // tk):
                kv = kvs_refs[gg][tk * c:tk * (c + 1), :].astype(F32)
                vst_ref[gg, c] = kv.T[HEAD_DIM:, :].astype(BF16)
                vwt_ref[gg, c] = kvw_refs[gg][tk * c:tk * (c + 1), :].astype(F32).T[HEAD_DIM:, :].astype(BF16)
                onehot = jnp.where(lane - HEAD_DIM == ((tk * c + krow) >> 6), 1.0, 0.0)
                kse_ref[gg, tk * c:tk * (c + 1), :] = jnp.where(lane < HEAD_DIM, kv, onehot).astype(BF16)
            vct_ref[gg] = kvc_ref[N_CMP_PAD * gg:N_CMP_PAD * (gg + 1), :].astype(F32).T[HEAD_DIM:, :].astype(BF16)

    kr = lax.broadcasted_iota(jnp.int32, (tk, tq), 0)
    tc = lax.broadcasted_iota(jnp.int32, (tk, tq), 1)
    diag = kr <= tc
    zpad = jnp.zeros((HEAD_DIM, tq), F32)

    qts = [q_ref[:, qw * gg:qw * (gg + 1)].astype(F32).T * (HEAD_DIM ** -0.5) for gg in groups]

    def stack_q(qt, extra):
        return jnp.concatenate(
            [jnp.concatenate([qt[HEAD_DIM * h:HEAD_DIM * (h + 1)], extra], axis=0) for h in range(HPG)],
            axis=1).astype(BF16)

    q4t = [stack_q(qts[gg], zpad) for gg in groups]

    win_s, win_vt, cmp_s = [], [], []
    for gg in groups:
        tiles, vts = [], []
        for u in range(win_tiles + 1):
            kt = qi - win_tiles + u
            exists = kt >= 0
            ktc = jnp.maximum(kt, 0)
            k0 = pl.multiple_of(ktc * tk, tk)
            if u == 0:
                mask = jnp.logical_and(kr > tc, exists)
            elif u < win_tiles:
                mask = jnp.logical_and(kr >= 0, exists)
            else:
                mask = diag
            tiles.append((kvw_refs[gg][pl.ds(k0, tk), :], mask))
            vts.append(vwt_ref[gg, ktc])
        win_s.append(_scores(tiles, q4t[gg], tq))
        win_vt.append(vts)
        cmp_s.append(_dot(kvc_ref[N_CMP_PAD * gg:N_CMP_PAD * (gg + 1), :], q4t[gg]))

    cmask = (kr * CMP_STRIDE + (CMP_LEN - 1)) <= (t0 + tc)
    o_cmp, imp4 = [], []
    for gg in groups:
        s = _mask_rows(cmp_s[gg], cmask, NEG_BIG, tq)
        m = jnp.max(s, axis=0, keepdims=True)
        e = _mask_rows(jnp.exp(s - m), cmask, 0.0, tq)
        den = jnp.sum(e, axis=0, keepdims=True)
        p = (e * (1.0 / jnp.maximum(den, 1e-30))).astype(BF16)
        o_cmp.append(_dot(vct_ref[gg], p))
        imp4.append(_dot(wt_ref[...], p))

    o_win = []
    for gg in groups:
        _, l_w, acc_w = _softmax_pv(None, win_s[gg], win_vt[gg])
        o_win.append(acc_w * (1.0 / l_w))

    j = lax.broadcasted_iota(jnp.int32, (N_SLC, tq), 0)
    blk = (t0 + lax.broadcasted_iota(jnp.int32, (N_SLC, tq), 1)) >> 6
    valid = j <= blk
    forced = (j == 0) | (j == blk) | (j == blk - 1)
    q4t_sel = []
    for gg in groups:
        imp = imp4[gg][:, 0:tq]
        for h in range(1, HPG):
            imp = imp + imp4[gg][:, tq * h:tq * (h + 1)]
        score = jnp.where(valid, imp + jnp.where(forced, FORCE_SCORE, 0.0), -FORCE_SCORE)
        rank = jnp.zeros(imp.shape, F32)
        for i in range(N_SLC):
            ri = score[i:i + 1, :]
            beats = jnp.where(ri > score, 1.0, jnp.where(ri == score, jnp.where(j > i, 1.0, 0.0), 0.0))
            rank = rank + beats
        sel_bias = jnp.where(valid, jnp.where(rank < float(SLC_TOPN), 0.0, NEG_BIG), NEG_BIG)
        q4t_sel.append(stack_q(qts[gg], jnp.concatenate([sel_bias, jnp.zeros((HEAD_DIM - N_SLC, tq), F32)], axis=0)))

    s_refs = (s0_ref, s1_ref)
    n_main = qi // unroll
    tail_base = n_main * unroll
    for gg in groups:
        for u in range(unroll):
            k0 = pl.multiple_of((tail_base + u) * tk, tk)
            s = _dot(kse_ref[gg, pl.ds(k0, tk), :], q4t_sel[gg])
            s0_ref[gg, u] = _mask_rows(s, (k0 + kr) <= (t0 + tc), NEG_BIG, tq)

    def process(parity):
        def run(args):
            i, carries = args
            base = jnp.where(i == 0, tail_base, (i - 1) * unroll)
            return tuple(
                _softmax_pv(carries[gg], [s_refs[parity][gg, u] for u in range(unroll)],
                            [vst_ref[gg, base + u] for u in range(unroll)]) for gg in groups)
        return run

    def step(parity):
        def run(args):
            i, carries = args
            for gg in groups:
                for u in range(unroll):
                    k0 = pl.multiple_of((i * unroll + u) * tk, tk)
                    s_refs[1 - parity][gg, u] = _dot(kse_ref[gg, pl.ds(k0, tk), :], q4t_sel[gg])
            return process(parity)(args)
        return run

    init = (jnp.full((1, lanes4), NEG_BIG, F32), jnp.zeros((1, lanes4), F32),
            jnp.zeros((HEAD_DIM, lanes4), F32))
    carries = lax.fori_loop(
        0, n_main, lambda i, c: lax.cond(i % 2 == 0, step(0), step(1), (i, c)), (init,) * ng)
    carries = lax.cond(n_main % 2 == 0, process(0), process(1), (n_main, carries))

    for gg in groups:
        _, l_s, acc_s = carries[gg]
        o_slc = acc_s * (1.0 / l_s)
        sg = jax.nn.sigmoid(gl_ref[:, LANES * gg:LANES * (gg + 1)].T[:BF16_SUBLANES, :])
        heads = []
        for h in range(HPG):
            sl = slice(tq * h, tq * (h + 1))
            r = N_BRANCH * h
            heads.append(sg[r:r + 1] * o_cmp[gg][:, sl] + sg[r + 1:r + 2] * o_slc[:, sl]
                         + sg[r + 2:r + 3] * o_win[gg][:, sl])
        o_ref[:, qw * gg:qw * (gg + 1)] = jnp.concatenate(heads, axis=0).T.astype(BF16)


def nsa_attention(main3, kvc, gates, wcs_t):
    b = main3.shape[0]
    ng = ATT_GROUPS
    tq = ATT_TILE
    nq = SEQ // tq
    qw = ng * HPG * HEAD_DIM
    kvs_base = Q_W // LANES
    kvw_base = kvs_base + KV_GROUPS

    def kv_spec(base, gg):
        return pl.BlockSpec((None, SEQ, LANES), lambda b_, gp, i: (b_, 0, base + gp * ng + gg))

    return pl.pallas_call(
        _nsa_attn_kernel,
        grid=(b, KV_GROUPS // ng, nq),
        in_specs=[pl.BlockSpec((None, tq, qw), lambda b_, gp, i: (b_, i, gp))]
        + [kv_spec(kvs_base, gg) for gg in range(ng)]
        + [kv_spec(kvw_base, gg) for gg in range(ng)]
        + [
            pl.BlockSpec((None, ng * N_CMP_PAD, LANES), lambda b_, gp, i: (b_, gp, 0)),
            pl.BlockSpec((tq, ng * LANES), lambda b_, gp, i: (b_ * nq + i, gp)),
            pl.BlockSpec((N_SLC, N_CMP_PAD), lambda b_, gp, i: (0, 0)),
        ],
        out_specs=pl.BlockSpec((None, tq, qw), lambda b_, gp, i: (b_, i, gp)),
        out_shape=jax.ShapeDtypeStruct((b, SEQ, Q_W), BF16),
        scratch_shapes=[
            pltpu.VMEM((ng, SEQ, LANES), BF16),
            pltpu.VMEM((ng, SEQ // tq, HEAD_DIM, tq), BF16),
            pltpu.VMEM((ng, SEQ // tq, HEAD_DIM, tq), BF16),
            pltpu.VMEM((ng, HEAD_DIM, N_CMP_PAD), BF16),
            pltpu.VMEM((ng, SLC_UNROLL, tq, HPG * tq), F32),
            pltpu.VMEM((ng, SLC_UNROLL, tq, HPG * tq), F32),
        ],
        compiler_params=_cparams(("parallel", "parallel", "arbitrary")),
        name="nsa_attention",
    )(main3, *([main3] * (2 * ng)), kvc, gates, wcs_t)


def _proj_norm_res_kernel(x_ref, a_ref, w_ref, g_ref, out_ref):
    out_ref[...] = x_ref[...] + _rms(_dot(a_ref[...], w_ref[...]), g_ref[...])


def proj_norm_res(x, a, w, g, *, tm):
    m, d = x.shape
    k = a.shape[1]
    row = pl.BlockSpec((tm, d), lambda i: (i, 0))
    return pl.pallas_call(
        _proj_norm_res_kernel,
        grid=(m // tm,),
        in_specs=[row, pl.BlockSpec((tm, k), lambda i: (i, 0)), pl.BlockSpec((k, d), lambda i: (0, 0)),
                  pl.BlockSpec((1, d), lambda i: (0, 0))],
        out_specs=row,
        out_shape=jax.ShapeDtypeStruct((m, d), F32),
        compiler_params=_cparams(("parallel",)),
        name="proj_norm_res",
    )(x, a, w, g)


def _sg_out_kernel(x_ref, u_ref, v_ref, lng_ref, lnb_ref, wsp_ref, bt_ref, wo_ref, g1_ref,
                   out_ref, wm_ref, gated_ref, *, tm):
    @pl.when(pl.program_id(0) == 0)
    def _():
        t = lax.broadcasted_iota(jnp.int32, (SG_CHUNK, SG_CHUNK), 0)
        s = lax.broadcasted_iota(jnp.int32, (SG_CHUNK, SG_CHUNK), 1)
        for g in range(SG_GROUPS):
            wm_ref[g] = jnp.where(s <= t, wsp_ref[g], 0.0).astype(BF16)

    v = v_ref[...]
    mu = jnp.mean(v, axis=-1, keepdims=True)
    vc = v - mu
    var = jnp.mean(vc * vc, axis=-1, keepdims=True)
    vn = (vc * lax.rsqrt(var + EPS) * lng_ref[...] + lnb_ref[...]).astype(BF16)
    for c in range(tm // SG_CHUNK):
        r0, r1 = SG_CHUNK * c, SG_CHUNK * (c + 1)
        for g in range(SG_GROUPS):
            c0, c1 = SG_GROUP_W * g, SG_GROUP_W * (g + 1)
            mixed = _dot(wm_ref[g], vn[r0:r1, c0:c1]) + bt_ref[:, g:g + 1]
            gated_ref[r0:r1, c0:c1] = (u_ref[r0:r1, c0:c1] * mixed).astype(BF16)
    m = _dot(gated_ref[...], wo_ref[...])
    out_ref[...] = x_ref[...] + _rms(m, g1_ref[...])


def sg_out(x, z, ln_g, ln_b, w_sp, b_sp_t, w_out, g1, *, tm):
    m, d = x.shape
    e = SG_WIDTH
    row = pl.BlockSpec((tm, d), lambda i: (i, 0))
    return pl.pallas_call(
        functools.partial(_sg_out_kernel, tm=tm),
        grid=(m // tm,),
        in_specs=[
            row,
            pl.BlockSpec((tm, e), lambda i: (i, 0)),
            pl.BlockSpec((tm, e), lambda i: (i, 1)),
            pl.BlockSpec((1, e), lambda i: (0, 0)),
            pl.BlockSpec((1, e), lambda i: (0, 0)),
            pl.BlockSpec((SG_GROUPS, SG_CHUNK, SG_CHUNK), lambda i: (0, 0, 0)),
            pl.BlockSpec((SG_CHUNK, SG_GROUPS), lambda i: (0, 0)),
            pl.BlockSpec((e, d), lambda i: (0, 0)),
            pl.BlockSpec((1, d), lambda i: (0, 0)),
        ],
        out_specs=row,
        out_shape=jax.ShapeDtypeStruct((m, d), F32),
        scratch_shapes=[pltpu.VMEM((SG_GROUPS, SG_CHUNK, SG_CHUNK), BF16), pltpu.VMEM((tm, e), BF16)],
        compiler_params=_cparams(("arbitrary",)),
        name="sg_out",
    )(x, z, z, ln_g, ln_b, w_sp, b_sp_t, w_out, g1)


HALO = BF16_SUBLANES
FFN_CHUNKS = ((0, 2), (2, 6), (6, 10), (10, 14), (14, 18), (18, 20), (20, 22))


def _ffn_kernel(x_ref, xh_ref, g2_ref, wg_ref, wv_ref, cwg_ref, cwv_ref, cbg_ref, cbv_ref,
                wd_ref, g3_ref, out_ref, xn_ref, y_ref, hg0_ref, hg1_ref, hv0_ref, hv1_ref,
                *, tm, tiles_per_seq, chunks):
    i = pl.program_id(0)
    phase_rows = tm // F32_SUBLANES
    d_blocks = D_MODEL // LANES

    xn_ref[HALO:, :] = _rms(x_ref[...], g2_ref[...]).astype(BF16)
    hn = _rms(xh_ref[...], g2_ref[...])
    seq_start = (i % tiles_per_seq) == 0
    xn_ref[:HALO, :] = jnp.where(seq_start, 0.0, hn).astype(BF16)

    hg_refs, hv_refs = (hg0_ref, hg1_ref), (hv0_ref, hv1_ref)

    def phases(h_ref, slot):
        return [h_ref[slot, pl.ds(HALO + p, phase_rows, stride=F32_SUBLANES), :]
                for p in range(1 - CONV_W, F32_SUBLANES)]

    def conv(ph, cw_ref, cb_ref, b, s):
        lo, hi = LANES * b, LANES * (b + 1)
        out = cb_ref[:, lo:hi]
        for k in range(CONV_W):
            out = out + cw_ref[k:k + 1, lo:hi] * ph[s + k]
        return out

    xa = xn_ref[...]

    def up_proj(n):
        b0, b1 = chunks[n]
        c0, c1 = LANES * b0, LANES * b1
        hg = _dot(xa, wg_ref[:, c0:c1])
        hv = _dot(xa, wv_ref[:, c0:c1])
        for b in range(b0, b1):
            hg_refs[n % 2][b - b0] = hg[:, LANES * (b - b0):LANES * (b - b0 + 1)]
            hv_refs[n % 2][b - b0] = hv[:, LANES * (b - b0):LANES * (b - b0 + 1)]

    def conv_act_down(n):
        b0, b1 = chunks[n]
        cols = []
        for b in range(b0, b1):
            pg = phases(hg_refs[n % 2], b - b0)
            pv = phases(hv_refs[n % 2], b - b0)
            rows = []
            for s in range(F32_SUBLANES):
                cg = conv(pg, cwg_ref, cbg_ref, b, s)
                cv = conv(pv, cwv_ref, cbv_ref, b, s)
                rows.append((jax.nn.silu(cg) * cv).astype(BF16))
            cols.append(jnp.concatenate(rows, axis=0))
        act = jnp.concatenate(cols, axis=1)
        return _dot(act, wd_ref[LANES * b0:LANES * b1, :])

    acc = None
    up_proj(0)
    for n in range(len(chunks)):
        if n + 1 < len(chunks):
            up_proj(n + 1)
        part = conv_act_down(n)
        acc = part if acc is None else acc + part

    y = _rms(acc, g3_ref[...])
    for cb in range(d_blocks):
        for s in range(F32_SUBLANES):
            y_ref[cb, pl.ds(s, phase_rows, stride=F32_SUBLANES), :] = (
                y[phase_rows * s:phase_rows * (s + 1), LANES * cb:LANES * (cb + 1)])
    out_ref[...] = x_ref[...] + jnp.concatenate([y_ref[cb] for cb in range(d_blocks)], axis=1)


def conv_ffn(x, g2, w_up, conv_w, conv_b, w_down, g3, *, tm, chunks):
    m, d = x.shape
    f = FFN_HIDDEN
    tiles_per_seq = SEQ // tm
    halo_blocks = tm // HALO
    chunk_blocks = max(b1 - b0 for b0, b1 in chunks)
    once = pl.Buffered(1)
    vec = pl.BlockSpec((1, d), lambda i: (0, 0))
    return pl.pallas_call(
        functools.partial(_ffn_kernel, tm=tm, tiles_per_seq=tiles_per_seq, chunks=chunks),
        grid=(m // tm,),
        in_specs=[
            pl.BlockSpec((tm, d), lambda i: (i, 0)),
            pl.BlockSpec((HALO, d), lambda i: (jnp.maximum(i * halo_blocks - 1, 0), 0)),
            vec,
            pl.BlockSpec((d, f), lambda i: (0, 0), pipeline_mode=once),
            pl.BlockSpec((d, f), lambda i: (0, 1), pipeline_mode=once),
            pl.BlockSpec((CONV_W, f), lambda i: (0, 0)),
            pl.BlockSpec((CONV_W, f), lambda i: (0, 1)),
            pl.BlockSpec((1, f), lambda i: (0, 0)),
            pl.BlockSpec((1, f), lambda i: (0, 1)),
            pl.BlockSpec((f, d), lambda i: (0, 0), pipeline_mode=once),
            vec,
        ],
        out_specs=pl.BlockSpec((tm, d), lambda i: (i, 0)),
        out_shape=jax.ShapeDtypeStruct((m, d), F32),
        scratch_shapes=[
            pltpu.VMEM((HALO + tm, d), BF16),
            pltpu.VMEM((d // LANES, tm, LANES), F32),
        ] + [pltpu.VMEM((chunk_blocks, HALO + tm, LANES), F32)] * 4,
        compiler_params=_cparams(("parallel",)),
        name="conv_ffn",
    )(x, x, g2, w_up, w_up, conv_w, conv_w, conv_b, conv_b, w_down, g3)


def _kv_pair_columns(pair):
    k_base = Q_W + (2 * pair) * KV_W
    v_base = Q_W + (2 * pair + 1) * KV_W
    order = []
    for g in range(KV_GROUPS):
        order += list(range(k_base + HEAD_DIM * g, k_base + HEAD_DIM * (g + 1)))
        order += list(range(v_base + HEAD_DIM * g, v_base + HEAD_DIM * (g + 1)))
    return order


def _nsa_in_weights(w_in):
    main = list(range(Q_W)) + _kv_pair_columns(1) + _kv_pair_columns(2)
    order = np.asarray(main + _kv_pair_columns(0), dtype=np.int32)
    w_gl = w_in[:, Q_W + 6 * KV_W:].reshape(D_MODEL, KV_GROUPS, HPG * N_BRANCH)
    w_gl = jnp.pad(w_gl, ((0, 0), (0, 0), (0, LANES - HPG * N_BRANCH))).reshape(D_MODEL, GATES_W)
    return jnp.concatenate([jnp.take(w_in, order, axis=1), w_gl], axis=1).astype(BF16)


def _cmp_to_slc_t():
    n_cmp = (SEQ - CMP_LEN) // CMP_STRIDE + 1
    cs = np.arange(n_cmp)[:, None] * CMP_STRIDE
    ss = np.arange(N_SLC)[None, :] * SLC_LEN
    ov = np.clip(np.minimum(cs + CMP_LEN, ss + SLC_LEN) - np.maximum(cs, ss), 0, None)
    w = ov.astype(np.float32) / np.float32(CMP_LEN)
    wt = np.zeros((N_SLC, N_CMP_PAD), dtype=np.float32)
    wt[:, :n_cmp] = w.T
    return jnp.asarray(wt, dtype=BF16)


def _block_diag2(a, b):
    za = jnp.zeros((a.shape[0], b.shape[1]), a.dtype)
    zb = jnp.zeros((b.shape[0], a.shape[1]), a.dtype)
    return jnp.concatenate([jnp.concatenate([a, za], axis=1), jnp.concatenate([zb, b], axis=1)], axis=0)


def _nsa_layer(x2, batch, g, w_in, cmp_pe, cmp_w1, cmp_w2, w_out):
    w1 = cmp_w1.reshape(2, CMP_LEN, HEAD_DIM, HEAD_DIM)
    w1bd = jnp.stack([_block_diag2(w1[0, l], w1[1, l]) for l in range(CMP_LEN)])
    wa = w1bd[:CMP_STRIDE].reshape(CMP_STRIDE * LANES, LANES).astype(BF16)
    wb = w1bd[CMP_STRIDE:].reshape(CMP_STRIDE * LANES, LANES).astype(BF16)
    w2bd = _block_diag2(cmp_w2[0], cmp_w2[1]).astype(BF16)
    pe_cat = jnp.concatenate([cmp_pe[0], cmp_pe[1]], axis=1)

    main, cmp, gates = nsa_in_proj(x2, g[0][None], _nsa_in_weights(w_in), tm=512)
    kvc = nsa_compress(cmp.reshape(batch, SEQ, CMP_W), pe_cat, wa, wb, w2bd)
    o = nsa_attention(main.reshape(batch, SEQ, MAIN_W), kvc, gates, _cmp_to_slc_t())
    return proj_norm_res(x2, o.reshape(-1, Q_W), w_out.astype(BF16), g[1][None], tm=512)


def _sg_layer(x2, g, w_in, ln_g, ln_b, w_sp, b_sp, w_out):
    z = norm_matmul_gelu(x2, g[0][None], w_in.astype(BF16), tm=512, chunk=512)
    return sg_out(x2, z, ln_g[None], ln_b[None], w_sp, b_sp.T, w_out.astype(BF16), g[1][None], tm=512)


def kernel(x, norm_gains, nsa_w_in, nsa_cmp_pe, nsa_cmp_w1, nsa_cmp_w2, nsa_w_out, sg_w_in, sg_ln_g,
           sg_ln_b, sg_w_sp, sg_b_sp, sg_w_out, ffn_w_up, ffn_conv_w, ffn_conv_b, ffn_w_down):
    batch, seq, d = x.shape
    assert (seq, d) == (SEQ, D_MODEL)
    x2 = x.reshape(batch * seq, d)
    for i in range(DEPTH):
        g = norm_gains[i]
        slot = i // N_MIXERS
        if i % N_MIXERS == 0:
            x2 = _nsa_layer(x2, batch, g, nsa_w_in[slot], nsa_cmp_pe[slot], nsa_cmp_w1[slot],
                            nsa_cmp_w2[slot], nsa_w_out[slot])
        else:
            x2 = _sg_layer(x2, g, sg_w_in[slot], sg_ln_g[slot], sg_ln_b[slot], sg_w_sp[slot],
                           sg_b_sp[slot], sg_w_out[slot])
        x2 = conv_ffn(x2, g[2][None], ffn_w_up[i].astype(BF16), ffn_conv_w[i], ffn_conv_b[i][None],
                      ffn_w_down[i].astype(BF16), g[3][None], tm=512, chunks=FFN_CHUNKS)
    return x2.reshape(batch, seq, d)
```

```python
import functools

import numpy as np
import jax
import jax.numpy as jnp
from jax import lax
from jax.experimental import pallas as pl
from jax.experimental.pallas import tpu as pltpu

D_MODEL = 1024
SEQ = 2048
DEPTH = 4
N_MIXERS = 2
HEADS = 16
KV_GROUPS = 4
HEAD_DIM = 64
HPG = HEADS // KV_GROUPS
KV_W = KV_GROUPS * HEAD_DIM
N_BRANCH = 3
CMP_LEN = 32
CMP_STRIDE = 16
SLC_LEN = 64
SLC_TOPN = 8
WINDOW = 512
FORCE_SCORE = 1.0e4
NEG_BIG = -1.0e30
SG_WIDTH = 2 * D_MODEL
SG_GROUPS = 16
SG_GROUP_W = SG_WIDTH // SG_GROUPS
SG_CHUNK = 128
FFN_HIDDEN = 2816
CONV_W = 3
EPS = 1e-6

N_SLC = SEQ // SLC_LEN
N_CMP_PAD = SEQ // CMP_STRIDE
Q_W = HEADS * HEAD_DIM
GATE_W = N_BRANCH * HEADS

LANES = 128
F32_SUBLANES = 8
BF16_SUBLANES = 16
VMEM_LIMIT = 48 * 1024 * 1024

KV_PAIR_W = KV_GROUPS * LANES
MAIN_W = Q_W + 2 * KV_PAIR_W
CMP_W = KV_PAIR_W
GATES_W = KV_GROUPS * LANES
ATT_TILE = 128
VT_ROWS = HEAD_DIM + BF16_SUBLANES
QK_SCALE = HEAD_DIM ** -0.5 * float(np.log2(np.e))
SLC_UNROLL = 4
ATT_GROUPS = 4

BF16 = jnp.bfloat16
F32 = jnp.float32


def _cparams(sem):
    return pltpu.CompilerParams(dimension_semantics=sem, vmem_limit_bytes=VMEM_LIMIT)


def _rms(x, g):
    return x * lax.rsqrt(jnp.mean(x * x, axis=-1, keepdims=True) + EPS) * g


def _dot(a, b):
    return jnp.dot(a, b, preferred_element_type=F32)


def _norm_matmul_gelu_kernel(x_ref, g_ref, w_ref, o_ref, *, chunk):
    xn = _rms(x_ref[...], g_ref[...]).astype(BF16)
    for c in range(w_ref.shape[1] // chunk):
        o_ref[:, chunk * c:chunk * (c + 1)] = jax.nn.gelu(_dot(xn, w_ref[:, chunk * c:chunk * (c + 1)]))


def norm_matmul_gelu(x, g, w, *, tm, chunk):
    m, k = x.shape
    n = w.shape[1]
    return pl.pallas_call(
        functools.partial(_norm_matmul_gelu_kernel, chunk=chunk),
        grid=(m // tm,),
        in_specs=[
            pl.BlockSpec((tm, k), lambda i: (i, 0)),
            pl.BlockSpec((1, k), lambda i: (0, 0)),
            pl.BlockSpec((k, n), lambda i: (0, 0), pipeline_mode=pl.Buffered(1)),
        ],
        out_specs=pl.BlockSpec((tm, n), lambda i: (i, 0)),
        out_shape=jax.ShapeDtypeStruct((m, n), F32),
        compiler_params=_cparams(("parallel",)),
        name="norm_matmul_gelu",
    )(x, g, w)


def _nsa_in_kernel(x_ref, g_ref, w_ref, main_ref, cmp_ref, gate_ref):
    xn = _rms(x_ref[...], g_ref[...]).astype(BF16)
    main_ref[...] = _dot(xn, w_ref[:, :MAIN_W]).astype(BF16)
    cmp_ref[...] = _dot(xn, w_ref[:, MAIN_W:MAIN_W + CMP_W])
    gate_ref[...] = _dot(xn, w_ref[:, MAIN_W + CMP_W:])


def nsa_in_proj(x, g, w_all, *, tm):
    m, d = x.shape
    n = w_all.shape[1]
    return pl.pallas_call(
        _nsa_in_kernel,
        grid=(m // tm,),
        in_specs=[
            pl.BlockSpec((tm, d), lambda i: (i, 0)),
            pl.BlockSpec((1, d), lambda i: (0, 0)),
            pl.BlockSpec((d, n), lambda i: (0, 0)),
        ],
        out_specs=[
            pl.BlockSpec((tm, MAIN_W), lambda i: (i, 0)),
            pl.BlockSpec((tm, CMP_W), lambda i: (i, 0)),
            pl.BlockSpec((tm, GATES_W), lambda i: (i, 0)),
        ],
        out_shape=[
            jax.ShapeDtypeStruct((m, MAIN_W), BF16),
            jax.ShapeDtypeStruct((m, CMP_W), F32),
            jax.ShapeDtypeStruct((m, GATES_W), F32),
        ],
        compiler_params=_cparams(("parallel",)),
        name="nsa_in_proj",
    )(x, g, w_all)


def _compress_kernel(x0_ref, x1_ref, x2_ref, x3_ref, pe_ref, wa_ref, wb_ref, w2_ref, kv_ref,
                     ca_ref, cb_ref):
    nck = N_CMP_PAD
    for l in range(CMP_STRIDE):
        for g, x_ref in enumerate((x0_ref, x1_ref, x2_ref, x3_ref)):
            xg = x_ref[pl.ds(l, nck, stride=CMP_STRIDE), :]
            ca_ref[nck * g:nck * (g + 1), LANES * l:LANES * (l + 1)] = (
                xg + pe_ref[l:l + 1, :]).astype(BF16)
            cb_ref[nck * g:nck * (g + 1), LANES * l:LANES * (l + 1)] = (
                xg + pe_ref[CMP_STRIDE + l:CMP_STRIDE + l + 1, :]).astype(BF16)
    a = _dot(ca_ref[...], wa_ref[...])
    bm = _dot(cb_ref[...], wb_ref[...])
    for g in range(KV_GROUPS):
        ag = a[nck * g:nck * (g + 1)]
        bg = bm[nck * g:nck * (g + 1)]
        pre = ag + pltpu.roll(bg, nck - 1, 0)
        h = jax.nn.gelu(pre).astype(BF16)
        kv_ref[nck * g:nck * (g + 1), :] = _dot(h, w2_ref[...]).astype(BF16)


def nsa_compress(cmp3, pe_cat, wa, wb, w2bd):
    b = cmp3.shape[0]
    rows = KV_GROUPS * N_CMP_PAD
    feat = CMP_STRIDE * LANES
    x_specs = [pl.BlockSpec((None, SEQ, LANES), functools.partial(lambda i, g: (i, 0, g), g=g))
               for g in range(KV_GROUPS)]
    return pl.pallas_call(
        _compress_kernel,
        grid=(b,),
        in_specs=x_specs + [
            pl.BlockSpec((CMP_LEN, LANES), lambda i: (0, 0)),
            pl.BlockSpec((feat, LANES), lambda i: (0, 0)),
            pl.BlockSpec((feat, LANES), lambda i: (0, 0)),
            pl.BlockSpec((LANES, LANES), lambda i: (0, 0)),
        ],
        out_specs=pl.BlockSpec((None, rows, LANES), lambda i: (i, 0, 0)),
        out_shape=jax.ShapeDtypeStruct((b, rows, LANES), BF16),
        scratch_shapes=[pltpu.VMEM((rows, feat), BF16), pltpu.VMEM((rows, feat), BF16)],
        compiler_params=_cparams(("parallel",)),
        name="nsa_compress",
    )(cmp3, cmp3, cmp3, cmp3, pe_cat, wa, wb, w2bd)


def _mask_rows(s, mask, fill, tq):
    return jnp.concatenate(
        [jnp.where(mask, s[:, tq * h:tq * (h + 1)], fill) for h in range(HPG)], axis=1)


def _scores(tiles, q4t, tq):
    out = []
    for lhs, mask in tiles:
        s = _dot(lhs, q4t)
        out.append(s if mask is None else _mask_rows(s, mask, NEG_BIG, tq))
    return out


def _softmax_pv(carry, ss, vts):
    m_new = None if carry is None else carry[0]
    for s in ss:
        mx = jnp.max(s, axis=0, keepdims=True)
        m_new = mx if m_new is None else jnp.maximum(m_new, mx)
    acc = None if carry is None else jnp.exp2(carry[0] - m_new) * carry[1]
    for s, vt in zip(ss, vts):
        pv = _dot(vt, jnp.exp2(s - m_new).astype(BF16))
        acc = pv if acc is None else acc + pv
    return m_new, acc


def _normalized(acc):
    return acc[:HEAD_DIM] * (1.0 / acc[HEAD_DIM:HEAD_DIM + 1])


def _nsa_attn_kernel(*refs):
    ng = ATT_GROUPS
    q_ref = refs[0]
    kvs_refs = refs[1:1 + ng]
    kvw_refs = refs[1 + ng:1 + 2 * ng]
    kvc_ref, gl_ref, wt_ref, o_ref, kse_ref, vst_ref, vwt_ref, vct_ref, s0_ref, s1_ref = refs[1 + 2 * ng:]
    tq = tk = ATT_TILE
    unroll = SLC_UNROLL
    qw = HPG * HEAD_DIM
    lanes4 = HPG * tq
    win_tiles = WINDOW // tk
    qi = pl.program_id(2)
    t0 = qi * tq
    groups = range(ng)

    @pl.when(qi == 0)
    def _():
        lane = lax.broadcasted_iota(jnp.int32, (tk, LANES), 1)
        krow = lax.broadcasted_iota(jnp.int32, (tk, LANES), 0)
        ones_pad = (lax.broadcasted_iota(jnp.int32, (VT_ROWS - HEAD_DIM, tk), 0) == 0).astype(F32)

        def vt_tile(kv):
            return jnp.concatenate([kv.T[HEAD_DIM:, :], ones_pad], axis=0).astype(BF16)

        for gg in groups:
            for c in range(SEQ // tk):
                kv = kvs_refs[gg][tk * c:tk * (c + 1), :].astype(F32)
                vst_ref[gg, c] = vt_tile(kv)
                vwt_ref[gg, c] = vt_tile(kvw_refs[gg][tk * c:tk * (c + 1), :].astype(F32))
                onehot = jnp.where(lane - HEAD_DIM == ((tk * c + krow) >> 6), 1.0, 0.0)
                kse_ref[gg, tk * c:tk * (c + 1), :] = jnp.where(lane < HEAD_DIM, kv, onehot).astype(BF16)
            vct_ref[gg] = kvc_ref[N_CMP_PAD * gg:N_CMP_PAD * (gg + 1), :].astype(F32).T[HEAD_DIM:, :].astype(BF16)

    kr = lax.broadcasted_iota(jnp.int32, (tk, tq), 0)
    tc = lax.broadcasted_iota(jnp.int32, (tk, tq), 1)
    diag = kr <= tc
    zpad = jnp.zeros((HEAD_DIM, tq), F32)

    qts = [q_ref[:, qw * gg:qw * (gg + 1)].astype(F32).T for gg in groups]

    def stack_q(qt, extra):
        return jnp.concatenate(
            [jnp.concatenate([qt[HEAD_DIM * h:HEAD_DIM * (h + 1)], extra], axis=0) for h in range(HPG)],
            axis=1).astype(BF16)

    q4t = [stack_q(qts[gg], zpad) for gg in groups]

    win_s, win_vt, cmp_s = [], [], []
    for gg in groups:
        tiles, vts = [], []
        for u in range(win_tiles + 1):
            kt = qi - win_tiles + u
            exists = kt >= 0
            ktc = jnp.maximum(kt, 0)
            k0 = pl.multiple_of(ktc * tk, tk)
            if u == 0:
                mask = jnp.logical_and(kr > tc, exists)
            elif u < win_tiles:
                mask = jnp.logical_and(kr >= 0, exists)
            else:
                mask = diag
            tiles.append((kvw_refs[gg][pl.ds(k0, tk), :], mask))
            vts.append(vwt_ref[gg, ktc])
        win_s.append(_scores(tiles, q4t[gg], tq))
        win_vt.append(vts)
        cmp_s.append(_dot(kvc_ref[N_CMP_PAD * gg:N_CMP_PAD * (gg + 1), :], q4t[gg]))

    cmask = (kr * CMP_STRIDE + (CMP_LEN - 1)) <= (t0 + tc)
    o_cmp, imp4 = [], []
    for gg in groups:
        s = _mask_rows(cmp_s[gg], cmask, NEG_BIG, tq)
        m = jnp.max(s, axis=0, keepdims=True)
        e = _mask_rows(jnp.exp2(s - m), cmask, 0.0, tq)
        den = jnp.sum(e, axis=0, keepdims=True)
        p = (e * (1.0 / jnp.maximum(den, 1e-30))).astype(BF16)
        o_cmp.append(_dot(vct_ref[gg], p))
        imp4.append(_dot(wt_ref[...], p))

    o_win = []
    for gg in groups:
        o_win.append(_normalized(_softmax_pv(None, win_s[gg], win_vt[gg])[1]))

    j = lax.broadcasted_iota(jnp.int32, (N_SLC, tq), 0)
    blk = (t0 + lax.broadcasted_iota(jnp.int32, (N_SLC, tq), 1)) >> 6
    valid = j <= blk
    forced = (j == 0) | (j == blk) | (j == blk - 1)
    q4t_sel = []
    for gg in groups:
        imp = imp4[gg][:, 0:tq]
        for h in range(1, HPG):
            imp = imp + imp4[gg][:, tq * h:tq * (h + 1)]
        score = jnp.where(valid, imp + jnp.where(forced, FORCE_SCORE, 0.0), -FORCE_SCORE)
        rank = jnp.zeros(imp.shape, F32)
        for i in range(N_SLC):
            ri = score[i:i + 1, :]
            beats = jnp.where(ri > score, 1.0, jnp.where(ri == score, jnp.where(j > i, 1.0, 0.0), 0.0))
            rank = rank + beats
        sel_bias = jnp.where(valid, jnp.where(rank < float(SLC_TOPN), 0.0, NEG_BIG), NEG_BIG)
        q4t_sel.append(stack_q(qts[gg], jnp.concatenate([sel_bias, jnp.zeros((HEAD_DIM - N_SLC, tq), F32)], axis=0)))

    s_refs = (s0_ref, s1_ref)
    n_main = qi // unroll
    tail_base = n_main * unroll
    for gg in groups:
        for u in range(unroll):
            k0 = pl.multiple_of((tail_base + u) * tk, tk)
            s = _dot(kse_ref[gg, pl.ds(k0, tk), :], q4t_sel[gg])
            s0_ref[gg, u] = _mask_rows(s, (k0 + kr) <= (t0 + tc), NEG_BIG, tq)

    def process(parity):
        def run(args):
            i, carries = args
            base = jnp.where(i == 0, tail_base, (i - 1) * unroll)
            return tuple(
                _softmax_pv(carries[gg], [s_refs[parity][gg, u] for u in range(unroll)],
                            [vst_ref[gg, base + u] for u in range(unroll)]) for gg in groups)
        return run

    def step(parity):
        def run(args):
            i, carries = args
            for gg in groups:
                for u in range(unroll):
                    k0 = pl.multiple_of((i * unroll + u) * tk, tk)
                    s_refs[1 - parity][gg, u] = _dot(kse_ref[gg, pl.ds(k0, tk), :], q4t_sel[gg])
            return process(parity)(args)
        return run

    init = (jnp.full((1, lanes4), NEG_BIG, F32), jnp.zeros((VT_ROWS, lanes4), F32))
    carries = lax.fori_loop(
        0, n_main, lambda i, c: lax.cond(i % 2 == 0, step(0), step(1), (i, c)), (init,) * ng)
    carries = lax.cond(n_main % 2 == 0, process(0), process(1), (n_main, carries))

    for gg in groups:
        o_slc = _normalized(carries[gg][1])
        sg = jax.nn.sigmoid(gl_ref[:, LANES * gg:LANES * (gg + 1)].T[:BF16_SUBLANES, :])
        heads = []
        for h in range(HPG):
            sl = slice(tq * h, tq * (h + 1))
            r = N_BRANCH * h
            heads.append(sg[r:r + 1] * o_cmp[gg][:, sl] + sg[r + 1:r + 2] * o_slc[:, sl]
                         + sg[r + 2:r + 3] * o_win[gg][:, sl])
        o_ref[:, qw * gg:qw * (gg + 1)] = jnp.concatenate(heads, axis=0).T.astype(BF16)


def nsa_attention(main3, kvc, gates, wcs_t):
    b = main3.shape[0]
    ng = ATT_GROUPS
    tq = ATT_TILE
    nq = SEQ // tq
    qw = ng * HPG * HEAD_DIM
    kvs_base = Q_W // LANES
    kvw_base = kvs_base + KV_GROUPS

    def kv_spec(base, gg):
        return pl.BlockSpec((None, SEQ, LANES), lambda b_, gp, i: (b_, 0, base + gp * ng + gg))

    return pl.pallas_call(
        _nsa_attn_kernel,
        grid=(b, KV_GROUPS // ng, nq),
        in_specs=[pl.BlockSpec((None, tq, qw), lambda b_, gp, i: (b_, i, gp))]
        + [kv_spec(kvs_base, gg) for gg in range(ng)]
        + [kv_spec(kvw_base, gg) for gg in range(ng)]
        + [
            pl.BlockSpec((None, ng * N_CMP_PAD, LANES), lambda b_, gp, i: (b_, gp, 0)),
            pl.BlockSpec((tq, ng * LANES), lambda b_, gp, i: (b_ * nq + i, gp)),
            pl.BlockSpec((N_SLC, N_CMP_PAD), lambda b_, gp, i: (0, 0)),
        ],
        out_specs=pl.BlockSpec((None, tq, qw), lambda b_, gp, i: (b_, i, gp)),
        out_shape=jax.ShapeDtypeStruct((b, SEQ, Q_W), BF16),
        scratch_shapes=[
            pltpu.VMEM((ng, SEQ, LANES), BF16),
            pltpu.VMEM((ng, SEQ // tq, VT_ROWS, tq), BF16),
            pltpu.VMEM((ng, SEQ // tq, VT_ROWS, tq), BF16),
            pltpu.VMEM((ng, HEAD_DIM, N_CMP_PAD), BF16),
            pltpu.VMEM((ng, SLC_UNROLL, tq, HPG * tq), F32),
            pltpu.VMEM((ng, SLC_UNROLL, tq, HPG * tq), F32),
        ],
        compiler_params=_cparams(("parallel", "parallel", "arbitrary")),
        name="nsa_attention",
    )(main3, *([main3] * (2 * ng)), kvc, gates, wcs_t)


def _proj_norm_res_kernel(x_ref, a_ref, w_ref, g_ref, out_ref):
    out_ref[...] = x_ref[...] + _rms(_dot(a_ref[...], w_ref[...]), g_ref[...])


def proj_norm_res(x, a, w, g, *, tm):
    m, d = x.shape
    k = a.shape[1]
    row = pl.BlockSpec((tm, d), lambda i: (i, 0))
    return pl.pallas_call(
        _proj_norm_res_kernel,
        grid=(m // tm,),
        in_specs=[row, pl.BlockSpec((tm, k), lambda i: (i, 0)), pl.BlockSpec((k, d), lambda i: (0, 0)),
                  pl.BlockSpec((1, d), lambda i: (0, 0))],
        out_specs=row,
        out_shape=jax.ShapeDtypeStruct((m, d), F32),
        compiler_params=_cparams(("parallel",)),
        name="proj_norm_res",
    )(x, a, w, g)


def _sg_out_kernel(x_ref, u_ref, v_ref, lng_ref, lnb_ref, wsp_ref, bt_ref, wo_ref, g1_ref,
                   out_ref, wm_ref, gated_ref, *, tm):
    @pl.when(pl.program_id(0) == 0)
    def _():
        t = lax.broadcasted_iota(jnp.int32, (SG_CHUNK, SG_CHUNK), 0)
        s = lax.broadcasted_iota(jnp.int32, (SG_CHUNK, SG_CHUNK), 1)
        for g in range(SG_GROUPS):
            wm_ref[g] = jnp.where(s <= t, wsp_ref[g], 0.0).astype(BF16)

    v = v_ref[...]
    mu = jnp.mean(v, axis=-1, keepdims=True)
    vc = v - mu
    var = jnp.mean(vc * vc, axis=-1, keepdims=True)
    vn = (vc * lax.rsqrt(var + EPS) * lng_ref[...] + lnb_ref[...]).astype(BF16)
    for c in range(tm // SG_CHUNK):
        r0, r1 = SG_CHUNK * c, SG_CHUNK * (c + 1)
        for g in range(SG_GROUPS):
            c0, c1 = SG_GROUP_W * g, SG_GROUP_W * (g + 1)
            mixed = _dot(wm_ref[g], vn[r0:r1, c0:c1]) + bt_ref[:, g:g + 1]
            gated_ref[r0:r1, c0:c1] = (u_ref[r0:r1, c0:c1] * mixed).astype(BF16)
    m = _dot(gated_ref[...], wo_ref[...])
    out_ref[...] = x_ref[...] + _rms(m, g1_ref[...])


def sg_out(x, z, ln_g, ln_b, w_sp, b_sp_t, w_out, g1, *, tm):
    m, d = x.shape
    e = SG_WIDTH
    row = pl.BlockSpec((tm, d), lambda i: (i, 0))
    return pl.pallas_call(
        functools.partial(_sg_out_kernel, tm=tm),
        grid=(m // tm,),
        in_specs=[
            row,
            pl.BlockSpec((tm, e), lambda i: (i, 0)),
            pl.BlockSpec((tm, e), lambda i: (i, 1)),
            pl.BlockSpec((1, e), lambda i: (0, 0)),
            pl.BlockSpec((1, e), lambda i: (0, 0)),
            pl.BlockSpec((SG_GROUPS, SG_CHUNK, SG_CHUNK), lambda i: (0, 0, 0)),
            pl.BlockSpec((SG_CHUNK, SG_GROUPS), lambda i: (0, 0)),
            pl.BlockSpec((e, d), lambda i: (0, 0)),
            pl.BlockSpec((1, d), lambda i: (0, 0)),
        ],
        out_specs=row,
        out_shape=jax.ShapeDtypeStruct((m, d), F32),
        scratch_shapes=[pltpu.VMEM((SG_GROUPS, SG_CHUNK, SG_CHUNK), BF16), pltpu.VMEM((tm, e), BF16)],
        compiler_params=_cparams(("arbitrary",)),
        name="sg_out",
    )(x, z, z, ln_g, ln_b, w_sp, b_sp_t, w_out, g1)


HALO = BF16_SUBLANES
FFN_CHUNKS = ((0, 2), (2, 6), (6, 10), (10, 14), (14, 18), (18, 20), (20, 22))


def _ffn_kernel(x_ref, xh_ref, g2_ref, wg_ref, wv_ref, cwg_ref, cwv_ref, cbg_ref, cbv_ref,
                wd_ref, g3_ref, out_ref, xn_ref, y_ref, hg0_ref, hg1_ref, hv0_ref, hv1_ref,
                *, tm, tiles_per_seq, chunks):
    i = pl.program_id(0)
    phase_rows = tm // F32_SUBLANES
    d_blocks = D_MODEL // LANES

    xn_ref[HALO:, :] = _rms(x_ref[...], g2_ref[...]).astype(BF16)
    hn = _rms(xh_ref[...], g2_ref[...])
    seq_start = (i % tiles_per_seq) == 0
    xn_ref[:HALO, :] = jnp.where(seq_start, 0.0, hn).astype(BF16)

    hg_refs, hv_refs = (hg0_ref, hg1_ref), (hv0_ref, hv1_ref)

    def phases(h_ref, slot):
        return [h_ref[slot, pl.ds(HALO + p, phase_rows, stride=F32_SUBLANES), :]
                for p in range(1 - CONV_W, F32_SUBLANES)]

    def conv(ph, cw_ref, cb_ref, b, s):
        lo, hi = LANES * b, LANES * (b + 1)
        out = cb_ref[:, lo:hi]
        for k in range(CONV_W):
            out = out + cw_ref[k:k + 1, lo:hi] * ph[s + k]
        return out

    xa = xn_ref[...]

    def up_proj(n):
        b0, b1 = chunks[n]
        c0, c1 = LANES * b0, LANES * b1
        hg = _dot(xa, wg_ref[:, c0:c1])
        hv = _dot(xa, wv_ref[:, c0:c1])
        for b in range(b0, b1):
            hg_refs[n % 2][b - b0] = hg[:, LANES * (b - b0):LANES * (b - b0 + 1)]
            hv_refs[n % 2][b - b0] = hv[:, LANES * (b - b0):LANES * (b - b0 + 1)]

    def conv_act_down(n):
        b0, b1 = chunks[n]
        cols = []
        for b in range(b0, b1):
            pg = phases(hg_refs[n % 2], b - b0)
            pv = phases(hv_refs[n % 2], b - b0)
            rows = []
            for s in range(F32_SUBLANES):
                cg = conv(pg, cwg_ref, cbg_ref, b, s)
                cv = conv(pv, cwv_ref, cbv_ref, b, s)
                rows.append((jax.nn.silu(cg) * cv).astype(BF16))
            cols.append(jnp.concatenate(rows, axis=0))
        act = jnp.concatenate(cols, axis=1)
        return _dot(act, wd_ref[LANES * b0:LANES * b1, :])

    acc = None
    up_proj(0)
    for n in range(len(chunks)):
        if n + 1 < len(chunks):
            up_proj(n + 1)
        part = conv_act_down(n)
        acc = part if acc is None else acc + part

    y = _rms(acc, g3_ref[...])
    for cb in range(d_blocks):
        for s in range(F32_SUBLANES):
            y_ref[cb, pl.ds(s, phase_rows, stride=F32_SUBLANES), :] = (
                y[phase_rows * s:phase_rows * (s + 1), LANES * cb:LANES * (cb + 1)])
    out_ref[...] = x_ref[...] + jnp.concatenate([y_ref[cb] for cb in range(d_blocks)], axis=1)


def conv_ffn(x, g2, w_up, conv_w, conv_b, w_down, g3, *, tm, chunks):
    m, d = x.shape
    f = FFN_HIDDEN
    tiles_per_seq = SEQ // tm
    halo_blocks = tm // HALO
    chunk_blocks = max(b1 - b0 for b0, b1 in chunks)
    once = pl.Buffered(1)
    vec = pl.BlockSpec((1, d), lambda i: (0, 0))
    return pl.pallas_call(
        functools.partial(_ffn_kernel, tm=tm, tiles_per_seq=tiles_per_seq, chunks=chunks),
        grid=(m // tm,),
        in_specs=[
            pl.BlockSpec((tm, d), lambda i: (i, 0)),
            pl.BlockSpec((HALO, d), lambda i: (jnp.maximum(i * halo_blocks - 1, 0), 0)),
            vec,
            pl.BlockSpec((d, f), lambda i: (0, 0), pipeline_mode=once),
            pl.BlockSpec((d, f), lambda i: (0, 1), pipeline_mode=once),
            pl.BlockSpec((CONV_W, f), lambda i: (0, 0)),
            pl.BlockSpec((CONV_W, f), lambda i: (0, 1)),
            pl.BlockSpec((1, f), lambda i: (0, 0)),
            pl.BlockSpec((1, f), lambda i: (0, 1)),
            pl.BlockSpec((f, d), lambda i: (0, 0), pipeline_mode=once),
            vec,
        ],
        out_specs=pl.BlockSpec((tm, d), lambda i: (i, 0)),
        out_shape=jax.ShapeDtypeStruct((m, d), F32),
        scratch_shapes=[
            pltpu.VMEM((HALO + tm, d), BF16),
            pltpu.VMEM((d // LANES, tm, LANES), F32),
        ] + [pltpu.VMEM((chunk_blocks, HALO + tm, LANES), F32)] * 4,
        compiler_params=_cparams(("parallel",)),
        name="conv_ffn",
    )(x, x, g2, w_up, w_up, conv_w, conv_w, conv_b, conv_b, w_down, g3)


def _regroup(w, pair):
    base = Q_W + 2 * pair * KV_W
    kv = w[:, base:base + 2 * KV_W].reshape(D_MODEL, 2, KV_GROUPS, HEAD_DIM)
    return jnp.transpose(kv, (0, 2, 1, 3)).reshape(D_MODEL, KV_PAIR_W)


def _nsa_in_weights(w_in):
    w_gl = w_in[:, Q_W + 6 * KV_W:].reshape(D_MODEL, KV_GROUPS, HPG * N_BRANCH)
    w_gl = jnp.pad(w_gl, ((0, 0), (0, 0), (0, LANES - HPG * N_BRANCH))).reshape(D_MODEL, GATES_W)
    return jnp.concatenate([w_in[:, :Q_W] * QK_SCALE, _regroup(w_in, 1), _regroup(w_in, 2),
                            _regroup(w_in, 0), w_gl], axis=1).astype(BF16)


def _cmp_to_slc_t():
    n_cmp = (SEQ - CMP_LEN) // CMP_STRIDE + 1
    cs = np.arange(n_cmp)[:, None] * CMP_STRIDE
    ss = np.arange(N_SLC)[None, :] * SLC_LEN
    ov = np.clip(np.minimum(cs + CMP_LEN, ss + SLC_LEN) - np.maximum(cs, ss), 0, None)
    w = ov.astype(np.float32) / np.float32(CMP_LEN)
    wt = np.zeros((N_SLC, N_CMP_PAD), dtype=np.float32)
    wt[:, :n_cmp] = w.T
    return jnp.asarray(wt, dtype=BF16)


def _block_diag2(a, b):
    za = jnp.zeros(a.shape[:-1] + (b.shape[-1],), a.dtype)
    zb = jnp.zeros(b.shape[:-1] + (a.shape[-1],), a.dtype)
    return jnp.concatenate([jnp.concatenate([a, za], axis=-1), jnp.concatenate([zb, b], axis=-1)], axis=-2)


def _nsa_layer(x2, batch, g, w_in, cmp_pe, cmp_w1, cmp_w2, w_out):
    w1 = cmp_w1.reshape(2, CMP_LEN, HEAD_DIM, HEAD_DIM)
    w1bd = _block_diag2(w1[0], w1[1])
    wa = w1bd[:CMP_STRIDE].reshape(CMP_STRIDE * LANES, LANES).astype(BF16)
    wb = w1bd[CMP_STRIDE:].reshape(CMP_STRIDE * LANES, LANES).astype(BF16)
    w2bd = _block_diag2(cmp_w2[0], cmp_w2[1]).astype(BF16)
    pe_cat = jnp.concatenate([cmp_pe[0], cmp_pe[1]], axis=1)

    main, cmp, gates = nsa_in_proj(x2, g[0][None], _nsa_in_weights(w_in), tm=512)
    kvc = nsa_compress(cmp.reshape(batch, SEQ, CMP_W), pe_cat, wa, wb, w2bd)
    o = nsa_attention(main.reshape(batch, SEQ, MAIN_W), kvc, gates, _cmp_to_slc_t())
    return proj_norm_res(x2, o.reshape(-1, Q_W), w_out.astype(BF16), g[1][None], tm=512)


def _sg_layer(x2, g, w_in, ln_g, ln_b, w_sp, b_sp, w_out):
    z = norm_matmul_gelu(x2, g[0][None], w_in.astype(BF16), tm=512, chunk=512)
    return sg_out(x2, z, ln_g[None], ln_b[None], w_sp, b_sp.T, w_out.astype(BF16), g[1][None], tm=512)


def kernel(x, norm_gains, nsa_w_in, nsa_cmp_pe, nsa_cmp_w1, nsa_cmp_w2, nsa_w_out, sg_w_in, sg_ln_g,
           sg_ln_b, sg_w_sp, sg_b_sp, sg_w_out, ffn_w_up, ffn_conv_w, ffn_conv_b, ffn_w_down):
    batch, seq, d = x.shape
    assert (seq, d) == (SEQ, D_MODEL)
    x2 = x.reshape(batch * seq, d)
    for i in range(DEPTH):
        g = norm_gains[i]
        slot = i // N_MIXERS
        if i % N_MIXERS == 0:
            x2 = _nsa_layer(x2, batch, g, nsa_w_in[slot], nsa_cmp_pe[slot], nsa_cmp_w1[slot],
                            nsa_cmp_w2[slot], nsa_w_out[slot])
        else:
            x2 = _sg_layer(x2, g, sg_w_in[slot], sg_ln_g[slot], sg_ln_b[slot], sg_w_sp[slot],
                           sg_b_sp[slot], sg_w_out[slot])
        x2 = conv_ffn(x2, g[2][None], ffn_w_up[i].astype(BF16), ffn_conv_w[i], ffn_conv_b[i][None],
                      ffn_w_down[i].astype(BF16), g[3][None], tm=512, chunks=FFN_CHUNKS)
    return x2.reshape(batch, seq, d)
```

```python
import functools

import numpy as np
import jax
import jax.numpy as jnp
from jax import lax
from jax.experimental import pallas as pl
from jax.experimental.pallas import tpu as pltpu

D_MODEL = 1024
SEQ = 2048
DEPTH = 4
N_MIXERS = 2
HEADS = 16
KV_GROUPS = 4
HEAD_DIM = 64
HPG = HEADS // KV_GROUPS
KV_W = KV_GROUPS * HEAD_DIM
N_BRANCH = 3
CMP_LEN = 32
CMP_STRIDE = 16
SLC_LEN = 64
SLC_TOPN = 8
WINDOW = 512
FORCE_SCORE = 1.0e4
NEG_BIG = -1.0e30
SG_WIDTH = 2 * D_MODEL
SG_GROUPS = 16
SG_GROUP_W = SG_WIDTH // SG_GROUPS
SG_CHUNK = 128
FFN_HIDDEN = 2816
CONV_W = 3
EPS = 1e-6

N_SLC = SEQ // SLC_LEN
N_CMP_PAD = SEQ // CMP_STRIDE
Q_W = HEADS * HEAD_DIM
GATE_W = N_BRANCH * HEADS

LANES = 128
F32_SUBLANES = 8
BF16_SUBLANES = 16
VMEM_LIMIT = 48 * 1024 * 1024

KV_PAIR_W = KV_GROUPS * LANES
MAIN_W = Q_W + 2 * KV_PAIR_W
CMP_W = KV_PAIR_W
GATES_W = KV_GROUPS * LANES
ATT_TILE = 128
VT_ROWS = HEAD_DIM + BF16_SUBLANES
QK_SCALE = HEAD_DIM ** -0.5 * float(np.log2(np.e))
SLC_UNROLL = 2
ATT_GROUPS = 4

BF16 = jnp.bfloat16
F32 = jnp.float32


def _cparams(sem):
    return pltpu.CompilerParams(dimension_semantics=sem, vmem_limit_bytes=VMEM_LIMIT)


def _rms(x, g):
    return x * lax.rsqrt(jnp.mean(x * x, axis=-1, keepdims=True) + EPS) * g


def _dot(a, b):
    return jnp.dot(a, b, preferred_element_type=F32)


def _norm_matmul_gelu_kernel(x_ref, g_ref, w_ref, o_ref, *, chunk):
    xn = _rms(x_ref[...], g_ref[...]).astype(BF16)
    for c in range(w_ref.shape[1] // chunk):
        o_ref[:, chunk * c:chunk * (c + 1)] = jax.nn.gelu(_dot(xn, w_ref[:, chunk * c:chunk * (c + 1)]))


def norm_matmul_gelu(x, g, w, *, tm, chunk):
    m, k = x.shape
    n = w.shape[1]
    return pl.pallas_call(
        functools.partial(_norm_matmul_gelu_kernel, chunk=chunk),
        grid=(m // tm,),
        in_specs=[
            pl.BlockSpec((tm, k), lambda i: (i, 0)),
            pl.BlockSpec((1, k), lambda i: (0, 0)),
            pl.BlockSpec((k, n), lambda i: (0, 0), pipeline_mode=pl.Buffered(1)),
        ],
        out_specs=pl.BlockSpec((tm, n), lambda i: (i, 0)),
        out_shape=jax.ShapeDtypeStruct((m, n), F32),
        compiler_params=_cparams(("parallel",)),
        name="norm_matmul_gelu",
    )(x, g, w)


def _nsa_in_kernel(x_ref, g_ref, w_ref, main_ref, cmp_ref, gate_ref):
    xn = _rms(x_ref[...], g_ref[...]).astype(BF16)
    main_ref[...] = _dot(xn, w_ref[:, :MAIN_W]).astype(BF16)
    cmp_ref[...] = _dot(xn, w_ref[:, MAIN_W:MAIN_W + CMP_W])
    gate_ref[...] = _dot(xn, w_ref[:, MAIN_W + CMP_W:])


def nsa_in_proj(x, g, w_all, *, tm):
    m, d = x.shape
    n = w_all.shape[1]
    return pl.pallas_call(
        _nsa_in_kernel,
        grid=(m // tm,),
        in_specs=[
            pl.BlockSpec((tm, d), lambda i: (i, 0)),
            pl.BlockSpec((1, d), lambda i: (0, 0)),
            pl.BlockSpec((d, n), lambda i: (0, 0)),
        ],
        out_specs=[
            pl.BlockSpec((tm, MAIN_W), lambda i: (i, 0)),
            pl.BlockSpec((tm, CMP_W), lambda i: (i, 0)),
            pl.BlockSpec((tm, GATES_W), lambda i: (i, 0)),
        ],
        out_shape=[
            jax.ShapeDtypeStruct((m, MAIN_W), BF16),
            jax.ShapeDtypeStruct((m, CMP_W), F32),
            jax.ShapeDtypeStruct((m, GATES_W), F32),
        ],
        compiler_params=_cparams(("parallel",)),
        name="nsa_in_proj",
    )(x, g, w_all)


def _compress_kernel(x0_ref, x1_ref, x2_ref, x3_ref, pe_ref, wa_ref, wb_ref, w2_ref, kv_ref,
                     ca_ref, cb_ref):
    nck = N_CMP_PAD
    for l in range(CMP_STRIDE):
        for g, x_ref in enumerate((x0_ref, x1_ref, x2_ref, x3_ref)):
            xg = x_ref[pl.ds(l, nck, stride=CMP_STRIDE), :]
            ca_ref[nck * g:nck * (g + 1), LANES * l:LANES * (l + 1)] = (
                xg + pe_ref[l:l + 1, :]).astype(BF16)
            cb_ref[nck * g:nck * (g + 1), LANES * l:LANES * (l + 1)] = (
                xg + pe_ref[CMP_STRIDE + l:CMP_STRIDE + l + 1, :]).astype(BF16)
    a = _dot(ca_ref[...], wa_ref[...])
    bm = _dot(cb_ref[...], wb_ref[...])
    for g in range(KV_GROUPS):
        ag = a[nck * g:nck * (g + 1)]
        bg = bm[nck * g:nck * (g + 1)]
        pre = ag + pltpu.roll(bg, nck - 1, 0)
        h = jax.nn.gelu(pre).astype(BF16)
        kv_ref[nck * g:nck * (g + 1), :] = _dot(h, w2_ref[...]).astype(BF16)


def nsa_compress(cmp3, pe_cat, wa, wb, w2bd):
    b = cmp3.shape[0]
    rows = KV_GROUPS * N_CMP_PAD
    feat = CMP_STRIDE * LANES
    x_specs = [pl.BlockSpec((None, SEQ, LANES), functools.partial(lambda i, g: (i, 0, g), g=g))
               for g in range(KV_GROUPS)]
    return pl.pallas_call(
        _compress_kernel,
        grid=(b,),
        in_specs=x_specs + [
            pl.BlockSpec((CMP_LEN, LANES), lambda i: (0, 0)),
            pl.BlockSpec((feat, LANES), lambda i: (0, 0)),
            pl.BlockSpec((feat, LANES), lambda i: (0, 0)),
            pl.BlockSpec((LANES, LANES), lambda i: (0, 0)),
        ],
        out_specs=pl.BlockSpec((None, rows, LANES), lambda i: (i, 0, 0)),
        out_shape=jax.ShapeDtypeStruct((b, rows, LANES), BF16),
        scratch_shapes=[pltpu.VMEM((rows, feat), BF16), pltpu.VMEM((rows, feat), BF16)],
        compiler_params=_cparams(("parallel",)),
        name="nsa_compress",
    )(cmp3, cmp3, cmp3, cmp3, pe_cat, wa, wb, w2bd)


def _mask_rows(s, mask, fill, tq):
    return jnp.concatenate(
        [jnp.where(mask, s[:, tq * h:tq * (h + 1)], fill) for h in range(HPG)], axis=1)


def _scores(tiles, q4t, tq):
    out = []
    for lhs, mask in tiles:
        s = _dot(lhs, q4t)
        out.append(s if mask is None else _mask_rows(s, mask, NEG_BIG, tq))
    return out


def _softmax_pv(carry, ss, vts):
    m_new = None if carry is None else carry[0]
    for s in ss:
        mx = jnp.max(s, axis=0, keepdims=True)
        m_new = mx if m_new is None else jnp.maximum(m_new, mx)
    acc = None if carry is None else jnp.exp2(carry[0] - m_new) * carry[1]
    for s, vt in zip(ss, vts):
        pv = _dot(vt, jnp.exp2(s - m_new).astype(BF16))
        acc = pv if acc is None else acc + pv
    return m_new, acc


def _normalized(acc):
    return acc[:HEAD_DIM] * (1.0 / acc[HEAD_DIM:HEAD_DIM + 1])


def _nsa_attn_kernel(*refs):
    ng = ATT_GROUPS
    q_ref = refs[0]
    kvs_refs = refs[1:1 + ng]
    kvw_refs = refs[1 + ng:1 + 2 * ng]
    kvc_ref, gl_ref, wt_ref, o_ref, kse_ref, vst_ref, vwt_ref, vct_ref, s0_ref, s1_ref = refs[1 + 2 * ng:]
    tq = tk = ATT_TILE
    unroll = SLC_UNROLL
    qw = HPG * HEAD_DIM
    lanes4 = HPG * tq
    win_tiles = WINDOW // tk
    qi = pl.program_id(2)
    t0 = qi * tq
    groups = range(ng)

    @pl.when(qi == 0)
    def _():
        lane = lax.broadcasted_iota(jnp.int32, (tk, LANES), 1)
        krow = lax.broadcasted_iota(jnp.int32, (tk, LANES), 0)
        ones_pad = (lax.broadcasted_iota(jnp.int32, (VT_ROWS - HEAD_DIM, tk), 0) == 0).astype(F32)

        def vt_tile(kv):
            return jnp.concatenate([kv.T[HEAD_DIM:, :], ones_pad], axis=0).astype(BF16)

        for gg in groups:
            for c in range(SEQ // tk):
                kv = kvs_refs[gg][tk * c:tk * (c + 1), :].astype(F32)
                vst_ref[gg, c] = vt_tile(kv)
                vwt_ref[gg, c] = vt_tile(kvw_refs[gg][tk * c:tk * (c + 1), :].astype(F32))
                onehot = jnp.where(lane - HEAD_DIM == ((tk * c + krow) >> 6), 1.0, 0.0)
                kse_ref[gg, tk * c:tk * (c + 1), :] = jnp.where(lane < HEAD_DIM, kv, onehot).astype(BF16)
            vct_ref[gg] = kvc_ref[N_CMP_PAD * gg:N_CMP_PAD * (gg + 1), :].astype(F32).T[HEAD_DIM:, :].astype(BF16)

    kr = lax.broadcasted_iota(jnp.int32, (tk, tq), 0)
    tc = lax.broadcasted_iota(jnp.int32, (tk, tq), 1)
    diag = kr <= tc
    zpad = jnp.zeros((HEAD_DIM, tq), F32)

    qts = [q_ref[:, qw * gg:qw * (gg + 1)].astype(F32).T for gg in groups]

    def stack_q(qt, extra):
        return jnp.concatenate(
            [jnp.concatenate([qt[HEAD_DIM * h:HEAD_DIM * (h + 1)], extra], axis=0) for h in range(HPG)],
            axis=1).astype(BF16)

    q4t = [stack_q(qts[gg], zpad) for gg in groups]

    win_s, win_vt, cmp_s = [], [], []
    for gg in groups:
        tiles, vts = [], []
        for u in range(win_tiles + 1):
            kt = qi - win_tiles + u
            exists = kt >= 0
            ktc = jnp.maximum(kt, 0)
            k0 = pl.multiple_of(ktc * tk, tk)
            if u == 0:
                mask = jnp.logical_and(kr > tc, exists)
            elif u < win_tiles:
                mask = jnp.logical_and(kr >= 0, exists)
            else:
                mask = diag
            tiles.append((kvw_refs[gg][pl.ds(k0, tk), :], mask))
            vts.append(vwt_ref[gg, ktc])
        win_s.append(_scores(tiles, q4t[gg], tq))
        win_vt.append(vts)
        cmp_s.append(_dot(kvc_ref[N_CMP_PAD * gg:N_CMP_PAD * (gg + 1), :], q4t[gg]))

    cmask = (kr * CMP_STRIDE + (CMP_LEN - 1)) <= (t0 + tc)
    o_cmp, imp4 = [], []
    for gg in groups:
        s = _mask_rows(cmp_s[gg], cmask, NEG_BIG, tq)
        m = jnp.max(s, axis=0, keepdims=True)
        e = _mask_rows(jnp.exp2(s - m), cmask, 0.0, tq)
        den = jnp.sum(e, axis=0, keepdims=True)
        p = (e * (1.0 / jnp.maximum(den, 1e-30))).astype(BF16)
        o_cmp.append(_dot(vct_ref[gg], p))
        imp4.append(_dot(wt_ref[...], p))

    o_win = []
    for gg in groups:
        o_win.append(_normalized(_softmax_pv(None, win_s[gg], win_vt[gg])[1]))

    j = lax.broadcasted_iota(jnp.int32, (N_SLC, tq), 0)
    blk = (t0 + lax.broadcasted_iota(jnp.int32, (N_SLC, tq), 1)) >> 6
    valid = j <= blk
    forced = (j == 0) | (j == blk) | (j == blk - 1)
    q4t_sel = []
    for gg in groups:
        imp = imp4[gg][:, 0:tq]
        for h in range(1, HPG):
            imp = imp + imp4[gg][:, tq * h:tq * (h + 1)]
        score = jnp.where(valid, imp + jnp.where(forced, FORCE_SCORE, 0.0), -FORCE_SCORE)
        slabs = [score[F32_SUBLANES * r:F32_SUBLANES * (r + 1)] for r in range(N_SLC // F32_SUBLANES)]
        ranks = [jnp.zeros(sl.shape, F32) for sl in slabs]
        for i in range(N_SLC):
            ri = score[i:i + 1, :]
            for r, sl in enumerate(slabs):
                ge = jnp.where(ri >= sl, 1.0, 0.0)
                gt = jnp.where(ri > sl, 1.0, 0.0)
                if F32_SUBLANES * r > i:
                    before = ge
                elif F32_SUBLANES * (r + 1) - 1 <= i:
                    before = gt
                else:
                    before = jnp.where(j[F32_SUBLANES * r:F32_SUBLANES * (r + 1)] > i, ge, gt)
                ranks[r] = ranks[r] + before
        rank = jnp.concatenate(ranks, axis=0)
        sel_bias = jnp.where(valid, jnp.where(rank < float(SLC_TOPN), 0.0, NEG_BIG), NEG_BIG)
        q4t_sel.append(stack_q(qts[gg], jnp.concatenate([sel_bias, jnp.zeros((HEAD_DIM - N_SLC, tq), F32)], axis=0)))

    s_refs = (s0_ref, s1_ref)
    n_main = qi // unroll
    tail_base = n_main * unroll
    for gg in groups:
        for u in range(unroll):
            k0 = pl.multiple_of((tail_base + u) * tk, tk)
            s = _dot(kse_ref[gg, pl.ds(k0, tk), :], q4t_sel[gg])
            s0_ref[gg, u] = _mask_rows(s, (k0 + kr) <= (t0 + tc), NEG_BIG, tq)

    def process(parity):
        def run(args):
            i, carries = args
            base = jnp.where(i == 0, tail_base, (i - 1) * unroll)
            return tuple(
                _softmax_pv(carries[gg], [s_refs[parity][gg, u] for u in range(unroll)],
                            [vst_ref[gg, base + u] for u in range(unroll)]) for gg in groups)
        return run

    def step(parity):
        def run(args):
            i, carries = args
            for gg in groups:
                for u in range(unroll):
                    k0 = pl.multiple_of((i * unroll + u) * tk, tk)
                    s_refs[1 - parity][gg, u] = _dot(kse_ref[gg, pl.ds(k0, tk), :], q4t_sel[gg])
            return process(parity)(args)
        return run

    init = (jnp.full((1, lanes4), NEG_BIG, F32), jnp.zeros((VT_ROWS, lanes4), F32))
    carries = lax.fori_loop(
        0, n_main, lambda i, c: lax.cond(i % 2 == 0, step(0), step(1), (i, c)), (init,) * ng)
    carries = lax.cond(n_main % 2 == 0, process(0), process(1), (n_main, carries))

    for gg in groups:
        o_slc = _normalized(carries[gg][1])
        sg = jax.nn.sigmoid(gl_ref[:, LANES * gg:LANES * (gg + 1)].T[:BF16_SUBLANES, :])
        heads = []
        for h in range(HPG):
            sl = slice(tq * h, tq * (h + 1))
            r = N_BRANCH * h
            heads.append(sg[r:r + 1] * o_cmp[gg][:, sl] + sg[r + 1:r + 2] * o_slc[:, sl]
                         + sg[r + 2:r + 3] * o_win[gg][:, sl])
        o_ref[:, qw * gg:qw * (gg + 1)] = jnp.concatenate(heads, axis=0).T.astype(BF16)


def nsa_attention(main3, kvc, gates, wcs_t):
    b = main3.shape[0]
    ng = ATT_GROUPS
    tq = ATT_TILE
    nq = SEQ // tq
    qw = ng * HPG * HEAD_DIM
    kvs_base = Q_W // LANES
    kvw_base = kvs_base + KV_GROUPS

    def kv_spec(base, gg):
        return pl.BlockSpec((None, SEQ, LANES), lambda b_, gp, i: (b_, 0, base + gp * ng + gg))

    return pl.pallas_call(
        _nsa_attn_kernel,
        grid=(b, KV_GROUPS // ng, nq),
        in_specs=[pl.BlockSpec((None, tq, qw), lambda b_, gp, i: (b_, i, gp))]
        + [kv_spec(kvs_base, gg) for gg in range(ng)]
        + [kv_spec(kvw_base, gg) for gg in range(ng)]
        + [
            pl.BlockSpec((None, ng * N_CMP_PAD, LANES), lambda b_, gp, i: (b_, gp, 0)),
            pl.BlockSpec((tq, ng * LANES), lambda b_, gp, i: (b_ * nq + i, gp)),
            pl.BlockSpec((N_SLC, N_CMP_PAD), lambda b_, gp, i: (0, 0)),
        ],
        out_specs=pl.BlockSpec((None, tq, qw), lambda b_, gp, i: (b_, i, gp)),
        out_shape=jax.ShapeDtypeStruct((b, SEQ, Q_W), BF16),
        scratch_shapes=[
            pltpu.VMEM((ng, SEQ, LANES), BF16),
            pltpu.VMEM((ng, SEQ // tq, VT_ROWS, tq), BF16),
            pltpu.VMEM((ng, SEQ // tq, VT_ROWS, tq), BF16),
            pltpu.VMEM((ng, HEAD_DIM, N_CMP_PAD), BF16),
            pltpu.VMEM((ng, SLC_UNROLL, tq, HPG * tq), F32),
            pltpu.VMEM((ng, SLC_UNROLL, tq, HPG * tq), F32),
        ],
        compiler_params=_cparams(("parallel", "parallel", "arbitrary")),
        name="nsa_attention",
    )(main3, *([main3] * (2 * ng)), kvc, gates, wcs_t)


def _proj_norm_res_kernel(x_ref, a_ref, w_ref, g_ref, out_ref):
    out_ref[...] = x_ref[...] + _rms(_dot(a_ref[...], w_ref[...]), g_ref[...])


def proj_norm_res(x, a, w, g, *, tm):
    m, d = x.shape
    k = a.shape[1]
    row = pl.BlockSpec((tm, d), lambda i: (i, 0))
    return pl.pallas_call(
        _proj_norm_res_kernel,
        grid=(m // tm,),
        in_specs=[row, pl.BlockSpec((tm, k), lambda i: (i, 0)), pl.BlockSpec((k, d), lambda i: (0, 0)),
                  pl.BlockSpec((1, d), lambda i: (0, 0))],
        out_specs=row,
        out_shape=jax.ShapeDtypeStruct((m, d), F32),
        compiler_params=_cparams(("parallel",)),
        name="proj_norm_res",
    )(x, a, w, g)


def _sg_out_kernel(x_ref, u_ref, v_ref, lng_ref, lnb_ref, wsp_ref, bt_ref, wo_ref, g1_ref,
                   out_ref, wm_ref, gated_ref, *, tm):
    @pl.when(pl.program_id(0) == 0)
    def _():
        t = lax.broadcasted_iota(jnp.int32, (SG_CHUNK, SG_CHUNK), 0)
        s = lax.broadcasted_iota(jnp.int32, (SG_CHUNK, SG_CHUNK), 1)
        for g in range(SG_GROUPS):
            wm_ref[g] = jnp.where(s <= t, wsp_ref[g], 0.0).astype(BF16)

    v = v_ref[...]
    mu = jnp.mean(v, axis=-1, keepdims=True)
    vc = v - mu
    var = jnp.mean(vc * vc, axis=-1, keepdims=True)
    vn = (vc * lax.rsqrt(var + EPS) * lng_ref[...] + lnb_ref[...]).astype(BF16)
    for c in range(tm // SG_CHUNK):
        r0, r1 = SG_CHUNK * c, SG_CHUNK * (c + 1)
        for g in range(SG_GROUPS):
            c0, c1 = SG_GROUP_W * g, SG_GROUP_W * (g + 1)
            mixed = _dot(wm_ref[g], vn[r0:r1, c0:c1]) + bt_ref[:, g:g + 1]
            gated_ref[r0:r1, c0:c1] = (u_ref[r0:r1, c0:c1] * mixed).astype(BF16)
    m = _dot(gated_ref[...], wo_ref[...])
    out_ref[...] = x_ref[...] + _rms(m, g1_ref[...])


def sg_out(x, z, ln_g, ln_b, w_sp, b_sp_t, w_out, g1, *, tm):
    m, d = x.shape
    e = SG_WIDTH
    row = pl.BlockSpec((tm, d), lambda i: (i, 0))
    return pl.pallas_call(
        functools.partial(_sg_out_kernel, tm=tm),
        grid=(m // tm,),
        in_specs=[
            row,
            pl.BlockSpec((tm, e), lambda i: (i, 0)),
            pl.BlockSpec((tm, e), lambda i: (i, 1)),
            pl.BlockSpec((1, e), lambda i: (0, 0)),
            pl.BlockSpec((1, e), lambda i: (0, 0)),
            pl.BlockSpec((SG_GROUPS, SG_CHUNK, SG_CHUNK), lambda i: (0, 0, 0)),
            pl.BlockSpec((SG_CHUNK, SG_GROUPS), lambda i: (0, 0)),
            pl.BlockSpec((e, d), lambda i: (0, 0)),
            pl.BlockSpec((1, d), lambda i: (0, 0)),
        ],
        out_specs=row,
        out_shape=jax.ShapeDtypeStruct((m, d), F32),
        scratch_shapes=[pltpu.VMEM((SG_GROUPS, SG_CHUNK, SG_CHUNK), BF16), pltpu.VMEM((tm, e), BF16)],
        compiler_params=_cparams(("arbitrary",)),
        name="sg_out",
    )(x, z, z, ln_g, ln_b, w_sp, b_sp_t, w_out, g1)


HALO = BF16_SUBLANES
FFN_CHUNKS = ((0, 2), (2, 6), (6, 10), (10, 14), (14, 18), (18, 20), (20, 22))


def _ffn_kernel(x_ref, xh_ref, g2_ref, wg_ref, wv_ref, cwg_ref, cwv_ref, cbg_ref, cbv_ref,
                wd_ref, g3_ref, out_ref, xn_ref, y_ref, hg0_ref, hg1_ref, hv0_ref, hv1_ref,
                *, tm, tiles_per_seq, chunks):
    i = pl.program_id(0)
    phase_rows = tm // F32_SUBLANES
    d_blocks = D_MODEL // LANES

    xn_ref[HALO:, :] = _rms(x_ref[...], g2_ref[...]).astype(BF16)
    hn = _rms(xh_ref[...], g2_ref[...])
    seq_start = (i % tiles_per_seq) == 0
    xn_ref[:HALO, :] = jnp.where(seq_start, 0.0, hn).astype(BF16)

    hg_refs, hv_refs = (hg0_ref, hg1_ref), (hv0_ref, hv1_ref)

    def phases(h_ref, slot):
        return [h_ref[slot, pl.ds(HALO + p, phase_rows, stride=F32_SUBLANES), :]
                for p in range(1 - CONV_W, F32_SUBLANES)]

    def conv(ph, cw_ref, cb_ref, b, s):
        lo, hi = LANES * b, LANES * (b + 1)
        out = cb_ref[:, lo:hi]
        for k in range(CONV_W):
            out = out + cw_ref[k:k + 1, lo:hi] * ph[s + k]
        return out

    xa = xn_ref[...]

    def up_proj(n):
        b0, b1 = chunks[n]
        c0, c1 = LANES * b0, LANES * b1
        hg = _dot(xa, wg_ref[:, c0:c1])
        hv = _dot(xa, wv_ref[:, c0:c1])
        for b in range(b0, b1):
            hg_refs[n % 2][b - b0] = hg[:, LANES * (b - b0):LANES * (b - b0 + 1)]
            hv_refs[n % 2][b - b0] = hv[:, LANES * (b - b0):LANES * (b - b0 + 1)]

    def conv_act_down(n):
        b0, b1 = chunks[n]
        cols = []
        for b in range(b0, b1):
            pg = phases(hg_refs[n % 2], b - b0)
            pv = phases(hv_refs[n % 2], b - b0)
            rows = []
            for s in range(F32_SUBLANES):
                cg = conv(pg, cwg_ref, cbg_ref, b, s)
                cv = conv(pv, cwv_ref, cbv_ref, b, s)
                rows.append((jax.nn.silu(cg) * cv).astype(BF16))
            cols.append(jnp.concatenate(rows, axis=0))
        act = jnp.concatenate(cols, axis=1)
        return _dot(act, wd_ref[LANES * b0:LANES * b1, :])

    acc = None
    up_proj(0)
    for n in range(len(chunks)):
        if n + 1 < len(chunks):
            up_proj(n + 1)
        part = conv_act_down(n)
        acc = part if acc is None else acc + part

    y = _rms(acc, g3_ref[...])
    for cb in range(d_blocks):
        for s in range(F32_SUBLANES):
            y_ref[cb, pl.ds(s, phase_rows, stride=F32_SUBLANES), :] = (
                y[phase_rows * s:phase_rows * (s + 1), LANES * cb:LANES * (cb + 1)])
    out_ref[...] = x_ref[...] + jnp.concatenate([y_ref[cb] for cb in range(d_blocks)], axis=1)


def conv_ffn(x, g2, w_up, conv_w, conv_b, w_down, g3, *, tm, chunks):
    m, d = x.shape
    f = FFN_HIDDEN
    tiles_per_seq = SEQ // tm
    halo_blocks = tm // HALO
    chunk_blocks = max(b1 - b0 for b0, b1 in chunks)
    once = pl.Buffered(1)
    vec = pl.BlockSpec((1, d), lambda i: (0, 0))
    return pl.pallas_call(
        functools.partial(_ffn_kernel, tm=tm, tiles_per_seq=tiles_per_seq, chunks=chunks),
        grid=(m // tm,),
        in_specs=[
            pl.BlockSpec((tm, d), lambda i: (i, 0)),
            pl.BlockSpec((HALO, d), lambda i: (jnp.maximum(i * halo_blocks - 1, 0), 0)),
            vec,
            pl.BlockSpec((d, f), lambda i: (0, 0), pipeline_mode=once),
            pl.BlockSpec((d, f), lambda i: (0, 1), pipeline_mode=once),
            pl.BlockSpec((CONV_W, f), lambda i: (0, 0)),
            pl.BlockSpec((CONV_W, f), lambda i: (0, 1)),
            pl.BlockSpec((1, f), lambda i: (0, 0)),
            pl.BlockSpec((1, f), lambda i: (0, 1)),
            pl.BlockSpec((f, d), lambda i: (0, 0), pipeline_mode=once),
            vec,
        ],
        out_specs=pl.BlockSpec((tm, d), lambda i: (i, 0)),
        out_shape=jax.ShapeDtypeStruct((m, d), F32),
        scratch_shapes=[
            pltpu.VMEM((HALO + tm, d), BF16),
            pltpu.VMEM((d // LANES, tm, LANES), F32),
        ] + [pltpu.VMEM((chunk_blocks, HALO + tm, LANES), F32)] * 4,
        compiler_params=_cparams(("parallel",)),
        name="conv_ffn",
    )(x, x, g2, w_up, w_up, conv_w, conv_w, conv_b, conv_b, w_down, g3)


def _regroup(w, pair):
    base = Q_W + 2 * pair * KV_W
    kv = w[:, base:base + 2 * KV_W].reshape(D_MODEL, 2, KV_GROUPS, HEAD_DIM)
    return jnp.transpose(kv, (0, 2, 1, 3)).reshape(D_MODEL, KV_PAIR_W)


def _nsa_in_weights(w_in):
    w_gl = w_in[:, Q_W + 6 * KV_W:].reshape(D_MODEL, KV_GROUPS, HPG * N_BRANCH)
    w_gl = jnp.pad(w_gl, ((0, 0), (0, 0), (0, LANES - HPG * N_BRANCH))).reshape(D_MODEL, GATES_W)
    return jnp.concatenate([w_in[:, :Q_W] * QK_SCALE, _regroup(w_in, 1), _regroup(w_in, 2),
                            _regroup(w_in, 0), w_gl], axis=1).astype(BF16)


def _cmp_to_slc_t():
    n_cmp = (SEQ - CMP_LEN) // CMP_STRIDE + 1
    cs = np.arange(n_cmp)[:, None] * CMP_STRIDE
    ss = np.arange(N_SLC)[None, :] * SLC_LEN
    ov = np.clip(np.minimum(cs + CMP_LEN, ss + SLC_LEN) - np.maximum(cs, ss), 0, None)
    w = ov.astype(np.float32) / np.float32(CMP_LEN)
    wt = np.zeros((N_SLC, N_CMP_PAD), dtype=np.float32)
    wt[:, :n_cmp] = w.T
    return jnp.asarray(wt, dtype=BF16)


def _block_diag2(a, b):
    za = jnp.zeros(a.shape[:-1] + (b.shape[-1],), a.dtype)
    zb = jnp.zeros(b.shape[:-1] + (a.shape[-1],), a.dtype)
    return jnp.concatenate([jnp.concatenate([a, za], axis=-1), jnp.concatenate([zb, b], axis=-1)], axis=-2)


def _nsa_layer(x2, batch, g, w_in, cmp_pe, cmp_w1, cmp_w2, w_out):
    w1 = cmp_w1.reshape(2, CMP_LEN, HEAD_DIM, HEAD_DIM)
    w1bd = _block_diag2(w1[0], w1[1])
    wa = w1bd[:CMP_STRIDE].reshape(CMP_STRIDE * LANES, LANES).astype(BF16)
    wb = w1bd[CMP_STRIDE:].reshape(CMP_STRIDE * LANES, LANES).astype(BF16)
    w2bd = _block_diag2(cmp_w2[0], cmp_w2[1]).astype(BF16)
    pe_cat = jnp.concatenate([cmp_pe[0], cmp_pe[1]], axis=1)

    main, cmp, gates = nsa_in_proj(x2, g[0][None], _nsa_in_weights(w_in), tm=512)
    kvc = nsa_compress(cmp.reshape(batch, SEQ, CMP_W), pe_cat, wa, wb, w2bd)
    o = nsa_attention(main.reshape(batch, SEQ, MAIN_W), kvc, gates, _cmp_to_slc_t())
    return proj_norm_res(x2, o.reshape(-1, Q_W), w_out.astype(BF16), g[1][None], tm=512)


def _sg_layer(x2, g, w_in, ln_g, ln_b, w_sp, b_sp, w_out):
    z = norm_matmul_gelu(x2, g[0][None], w_in.astype(BF16), tm=512, chunk=512)
    return sg_out(x2, z, ln_g[None], ln_b[None], w_sp, b_sp.T, w_out.astype(BF16), g[1][None], tm=512)


def kernel(x, norm_gains, nsa_w_in, nsa_cmp_pe, nsa_cmp_w1, nsa_cmp_w2, nsa_w_out, sg_w_in, sg_ln_g,
           sg_ln_b, sg_w_sp, sg_b_sp, sg_w_out, ffn_w_up, ffn_conv_w, ffn_conv_b, ffn_w_down):
    batch, seq, d = x.shape
    assert (seq, d) == (SEQ, D_MODEL)
    x2 = x.reshape(batch * seq, d)
    for i in range(DEPTH):
        g = norm_gains[i]
        slot = i // N_MIXERS
        if i % N_MIXERS == 0:
            x2 = _nsa_layer(x2, batch, g, nsa_w_in[slot], nsa_cmp_pe[slot], nsa_cmp_w1[slot],
                            nsa_cmp_w2[slot], nsa_w_out[slot])
        else:
            x2 = _sg_layer(x2, g, sg_w_in[slot], sg_ln_g[slot], sg_ln_b[slot], sg_w_sp[slot],
                           sg_b_sp[slot], sg_w_out[slot])
        x2 = conv_ffn(x2, g[2][None], ffn_w_up[i].astype(BF16), ffn_conv_w[i], ffn_conv_b[i][None],
                      ffn_w_down[i].astype(BF16), g[3][None], tm=512, chunks=FFN_CHUNKS)
    return x2.reshape(batch, seq, d)
```

```python
import functools

import numpy as np
import jax
import jax.numpy as jnp
from jax import lax
from jax.experimental import pallas as pl
from jax.experimental.pallas import tpu as pltpu

D_MODEL = 1024
SEQ = 2048
DEPTH = 4
N_MIXERS = 2
HEADS = 16
KV_GROUPS = 4
HEAD_DIM = 64
HPG = HEADS // KV_GROUPS
KV_W = KV_GROUPS * HEAD_DIM
N_BRANCH = 3
CMP_LEN = 32
CMP_STRIDE = 16
SLC_LEN = 64
SLC_TOPN = 8
WINDOW = 512
FORCE_SCORE = 1.0e4
NEG_BIG = -1.0e30
SG_WIDTH = 2 * D_MODEL
SG_GROUPS = 16
SG_GROUP_W = SG_WIDTH // SG_GROUPS
SG_CHUNK = 128
FFN_HIDDEN = 2816
CONV_W = 3
EPS = 1e-6

N_SLC = SEQ // SLC_LEN
SLC_SHIFT = SLC_LEN.bit_length() - 1
N_CMP_PAD = SEQ // CMP_STRIDE
Q_W = HEADS * HEAD_DIM

LANES = 128
F32_SUBLANES = 8
BF16_SUBLANES = 16
VMEM_LIMIT = 48 * 1024 * 1024

KV_PAIR_W = KV_GROUPS * LANES
MAIN_W = Q_W + 2 * KV_PAIR_W
CMP_W = KV_PAIR_W
GATES_W = KV_GROUPS * LANES
ATT_TILE = 128
VT_ROWS = HEAD_DIM + BF16_SUBLANES
QK_SCALE = HEAD_DIM ** -0.5 * float(np.log2(np.e))
SLC_UNROLL = 4
ATT_GROUPS = 4

BF16 = jnp.bfloat16
F32 = jnp.float32


def _cparams(sem):
    return pltpu.CompilerParams(dimension_semantics=sem, vmem_limit_bytes=VMEM_LIMIT)


def _rms(x, g):
    return x * lax.rsqrt(jnp.mean(x * x, axis=-1, keepdims=True) + EPS) * g


def _dot(a, b):
    return jnp.dot(a, b, preferred_element_type=F32)


def _norm_matmul_gelu_kernel(x_ref, g_ref, w_ref, o_ref, *, chunk):
    xn = _rms(x_ref[...], g_ref[...]).astype(BF16)
    for c in range(w_ref.shape[1] // chunk):
        o_ref[:, chunk * c:chunk * (c + 1)] = jax.nn.gelu(_dot(xn, w_ref[:, chunk * c:chunk * (c + 1)]))


def norm_matmul_gelu(x, g, w, *, tm, chunk):
    m, k = x.shape
    n = w.shape[1]
    return pl.pallas_call(
        functools.partial(_norm_matmul_gelu_kernel, chunk=chunk),
        grid=(m // tm,),
        in_specs=[
            pl.BlockSpec((tm, k), lambda i: (i, 0)),
            pl.BlockSpec((1, k), lambda i: (0, 0)),
            pl.BlockSpec((k, n), lambda i: (0, 0), pipeline_mode=pl.Buffered(1)),
        ],
        out_specs=pl.BlockSpec((tm, n), lambda i: (i, 0)),
        out_shape=jax.ShapeDtypeStruct((m, n), F32),
        compiler_params=_cparams(("parallel",)),
        name="norm_matmul_gelu",
    )(x, g, w)


def _nsa_in_kernel(x_ref, g_ref, w_ref, main_ref, cmp_ref, gate_ref):
    xn = _rms(x_ref[...], g_ref[...]).astype(BF16)
    main_ref[...] = _dot(xn, w_ref[:, :MAIN_W]).astype(BF16)
    cmp_ref[...] = _dot(xn, w_ref[:, MAIN_W:MAIN_W + CMP_W])
    gate_ref[...] = _dot(xn, w_ref[:, MAIN_W + CMP_W:])


def nsa_in_proj(x, g, w_all, *, tm):
    m, d = x.shape
    n = w_all.shape[1]
    return pl.pallas_call(
        _nsa_in_kernel,
        grid=(m // tm,),
        in_specs=[
            pl.BlockSpec((tm, d), lambda i: (i, 0)),
            pl.BlockSpec((1, d), lambda i: (0, 0)),
            pl.BlockSpec((d, n), lambda i: (0, 0)),
        ],
        out_specs=[
            pl.BlockSpec((tm, MAIN_W), lambda i: (i, 0)),
            pl.BlockSpec((tm, CMP_W), lambda i: (i, 0)),
            pl.BlockSpec((tm, GATES_W), lambda i: (i, 0)),
        ],
        out_shape=[
            jax.ShapeDtypeStruct((m, MAIN_W), BF16),
            jax.ShapeDtypeStruct((m, CMP_W), F32),
            jax.ShapeDtypeStruct((m, GATES_W), F32),
        ],
        compiler_params=_cparams(("parallel",)),
        name="nsa_in_proj",
    )(x, g, w_all)


def _compress_kernel(x0_ref, x1_ref, x2_ref, x3_ref, pe_ref, wa_ref, wb_ref, w2_ref, kv_ref,
                     ca_ref, cb_ref):
    nck = N_CMP_PAD
    for l in range(CMP_STRIDE):
        for g, x_ref in enumerate((x0_ref, x1_ref, x2_ref, x3_ref)):
            xg = x_ref[pl.ds(l, nck, stride=CMP_STRIDE), :]
            ca_ref[nck * g:nck * (g + 1), LANES * l:LANES * (l + 1)] = (
                xg + pe_ref[l:l + 1, :]).astype(BF16)
            cb_ref[nck * g:nck * (g + 1), LANES * l:LANES * (l + 1)] = (
                xg + pe_ref[CMP_STRIDE + l:CMP_STRIDE + l + 1, :]).astype(BF16)
    a = _dot(ca_ref[...], wa_ref[...])
    bm = _dot(cb_ref[...], wb_ref[...])
    for g in range(KV_GROUPS):
        ag = a[nck * g:nck * (g + 1)]
        bg = bm[nck * g:nck * (g + 1)]
        pre = ag + pltpu.roll(bg, nck - 1, 0)
        h = jax.nn.gelu(pre).astype(BF16)
        kv_ref[nck * g:nck * (g + 1), :] = _dot(h, w2_ref[...]).astype(BF16)


def nsa_compress(cmp3, pe_cat, wa, wb, w2bd):
    b = cmp3.shape[0]
    rows = KV_GROUPS * N_CMP_PAD
    feat = CMP_STRIDE * LANES
    x_specs = [pl.BlockSpec((None, SEQ, LANES), functools.partial(lambda i, g: (i, 0, g), g=g))
               for g in range(KV_GROUPS)]
    return pl.pallas_call(
        _compress_kernel,
        grid=(b,),
        in_specs=x_specs + [
            pl.BlockSpec((CMP_LEN, LANES), lambda i: (0, 0)),
            pl.BlockSpec((feat, LANES), lambda i: (0, 0)),
            pl.BlockSpec((feat, LANES), lambda i: (0, 0)),
            pl.BlockSpec((LANES, LANES), lambda i: (0, 0)),
        ],
        out_specs=pl.BlockSpec((None, rows, LANES), lambda i: (i, 0, 0)),
        out_shape=jax.ShapeDtypeStruct((b, rows, LANES), BF16),
        scratch_shapes=[pltpu.VMEM((rows, feat), BF16), pltpu.VMEM((rows, feat), BF16)],
        compiler_params=_cparams(("parallel",)),
        name="nsa_compress",
    )(cmp3, cmp3, cmp3, cmp3, pe_cat, wa, wb, w2bd)


def _mask_rows(s, mask, fill, tq):
    return jnp.concatenate(
        [jnp.where(mask, s[:, tq * h:tq * (h + 1)], fill) for h in range(HPG)], axis=1)


def _scores(tiles, q4t, tq):
    out = []
    for lhs, mask in tiles:
        s = _dot(lhs, q4t)
        out.append(s if mask is None else _mask_rows(s, mask, NEG_BIG, tq))
    return out


def _softmax_pv(carry, ss, vts):
    m_new = None if carry is None else carry[0]
    for s in ss:
        mx = jnp.max(s, axis=0, keepdims=True)
        m_new = mx if m_new is None else jnp.maximum(m_new, mx)
    acc = None if carry is None else jnp.exp2(carry[0] - m_new) * carry[1]
    for s, vt in zip(ss, vts):
        pv = _dot(vt, jnp.exp2(s - m_new).astype(BF16))
        acc = pv if acc is None else acc + pv
    return m_new, acc


def _normalized(acc):
    return acc[:HEAD_DIM] * (1.0 / acc[HEAD_DIM:HEAD_DIM + 1])


def _nsa_attn_kernel(*refs):
    ng = ATT_GROUPS
    q_ref = refs[0]
    kvs_refs = refs[1:1 + ng]
    kvw_refs = refs[1 + ng:1 + 2 * ng]
    kvc_ref, gl_ref, wt_ref, o_ref, kse_ref, vst_ref, vwt_ref, vct_ref, s0_ref, s1_ref = refs[1 + 2 * ng:]
    tq = tk = ATT_TILE
    unroll = SLC_UNROLL
    qw = HPG * HEAD_DIM
    lanes4 = HPG * tq
    win_tiles = WINDOW // tk
    qi = pl.program_id(2)
    t0 = qi * tq
    groups = range(ng)

    @pl.when(qi == 0)
    def _():
        lane = lax.broadcasted_iota(jnp.int32, (tk, LANES), 1)
        krow = lax.broadcasted_iota(jnp.int32, (tk, LANES), 0)
        ones_pad = (lax.broadcasted_iota(jnp.int32, (VT_ROWS - HEAD_DIM, tk), 0) == 0).astype(F32)

        def vt_tile(kv):
            return jnp.concatenate([kv.T[HEAD_DIM:, :], ones_pad], axis=0).astype(BF16)

        for gg in groups:
            for c in range(SEQ // tk):
                kv = kvs_refs[gg][tk * c:tk * (c + 1), :].astype(F32)
                vst_ref[gg, c] = vt_tile(kv)
                vwt_ref[gg, c] = vt_tile(kvw_refs[gg][tk * c:tk * (c + 1), :].astype(F32))
                onehot = jnp.where(lane - HEAD_DIM == ((tk * c + krow) >> SLC_SHIFT), 1.0, 0.0)
                kse_ref[gg, tk * c:tk * (c + 1), :] = jnp.where(lane < HEAD_DIM, kv, onehot).astype(BF16)
            vct_ref[gg] = kvc_ref[N_CMP_PAD * gg:N_CMP_PAD * (gg + 1), :].astype(F32).T[HEAD_DIM:, :].astype(BF16)

    kr = lax.broadcasted_iota(jnp.int32, (tk, tq), 0)
    tc = lax.broadcasted_iota(jnp.int32, (tk, tq), 1)
    diag = kr <= tc
    zpad = jnp.zeros((HEAD_DIM, tq), F32)

    qts = [q_ref[:, qw * gg:qw * (gg + 1)].astype(F32).T for gg in groups]

    def stack_q(qt, extra):
        return jnp.concatenate(
            [jnp.concatenate([qt[HEAD_DIM * h:HEAD_DIM * (h + 1)], extra], axis=0) for h in range(HPG)],
            axis=1).astype(BF16)

    q4t = [stack_q(qts[gg], zpad) for gg in groups]

    win_s, win_vt, cmp_s = [], [], []
    for gg in groups:
        tiles, vts = [], []
        for u in range(win_tiles + 1):
            kt = qi - win_tiles + u
            exists = kt >= 0
            ktc = jnp.maximum(kt, 0)
            k0 = pl.multiple_of(ktc * tk, tk)
            if u == 0:
                mask = jnp.logical_and(kr > tc, exists)
            elif u < win_tiles:
                mask = jnp.logical_and(kr >= 0, exists)
            else:
                mask = diag
            tiles.append((kvw_refs[gg][pl.ds(k0, tk), :], mask))
            vts.append(vwt_ref[gg, ktc])
        win_s.append(_scores(tiles, q4t[gg], tq))
        win_vt.append(vts)
        cmp_s.append(_dot(kvc_ref[N_CMP_PAD * gg:N_CMP_PAD * (gg + 1), :], q4t[gg]))

    cmask = (kr * CMP_STRIDE + (CMP_LEN - 1)) <= (t0 + tc)
    o_cmp, imp4 = [], []
    for gg in groups:
        s = _mask_rows(cmp_s[gg], cmask, NEG_BIG, tq)
        m = jnp.max(s, axis=0, keepdims=True)
        e = _mask_rows(jnp.exp2(s - m), cmask, 0.0, tq)
        den = jnp.sum(e, axis=0, keepdims=True)
        p = (e * (1.0 / jnp.maximum(den, 1e-30))).astype(BF16)
        o_cmp.append(_dot(vct_ref[gg], p))
        imp4.append(_dot(wt_ref[...], p))

    o_win = []
    for gg in groups:
        o_win.append(_normalized(_softmax_pv(None, win_s[gg], win_vt[gg])[1]))

    j = lax.broadcasted_iota(jnp.int32, (N_SLC, tq), 0)
    blk = (t0 + lax.broadcasted_iota(jnp.int32, (N_SLC, tq), 1)) >> SLC_SHIFT
    valid = j <= blk
    forced = (j == 0) | (j == blk) | (j == blk - 1)
    q4t_sel = []
    for gg in groups:
        imp = imp4[gg][:, 0:tq]
        for h in range(1, HPG):
            imp = imp + imp4[gg][:, tq * h:tq * (h + 1)]
        score = jnp.where(valid, imp + jnp.where(forced, FORCE_SCORE, 0.0), -FORCE_SCORE)
        rank = jnp.zeros(imp.shape, F32)
        for i in range(N_SLC):
            ri = score[i:i + 1, :]
            beats = jnp.where(ri > score, 1.0, jnp.where(ri == score, jnp.where(j > i, 1.0, 0.0), 0.0))
            rank = rank + beats
        sel_bias = jnp.where(valid, jnp.where(rank < float(SLC_TOPN), 0.0, NEG_BIG), NEG_BIG)
        q4t_sel.append(stack_q(qts[gg], jnp.concatenate([sel_bias, jnp.zeros((HEAD_DIM - N_SLC, tq), F32)], axis=0)))

    s_refs = (s0_ref, s1_ref)
    n_main = qi // unroll
    tail_base = n_main * unroll
    for gg in groups:
        for u in range(unroll):
            k0 = pl.multiple_of((tail_base + u) * tk, tk)
            s = _dot(kse_ref[gg, pl.ds(k0, tk), :], q4t_sel[gg])
            s0_ref[gg, u] = _mask_rows(s, (k0 + kr) <= (t0 + tc), NEG_BIG, tq)

    def process(parity):
        def run(args):
            i, carries = args
            base = jnp.where(i == 0, tail_base, (i - 1) * unroll)
            return tuple(
                _softmax_pv(carries[gg], [s_refs[parity][gg, u] for u in range(unroll)],
                            [vst_ref[gg, base + u] for u in range(unroll)]) for gg in groups)
        return run

    def step(parity):
        def run(args):
            i, carries = args
            for gg in groups:
                for u in range(unroll):
                    k0 = pl.multiple_of((i * unroll + u) * tk, tk)
                    s_refs[1 - parity][gg, u] = _dot(kse_ref[gg, pl.ds(k0, tk), :], q4t_sel[gg])
            return process(parity)(args)
        return run

    init = (jnp.full((1, lanes4), NEG_BIG, F32), jnp.zeros((VT_ROWS, lanes4), F32))
    carries = lax.fori_loop(
        0, n_main, lambda i, c: lax.cond(i % 2 == 0, step(0), step(1), (i, c)), (init,) * ng)
    carries = lax.cond(n_main % 2 == 0, process(0), process(1), (n_main, carries))

    for gg in groups:
        o_slc = _normalized(carries[gg][1])
        sg = jax.nn.sigmoid(gl_ref[:, LANES * gg:LANES * (gg + 1)].T[:BF16_SUBLANES, :])
        heads = []
        for h in range(HPG):
            sl = slice(tq * h, tq * (h + 1))
            r = N_BRANCH * h
            heads.append(sg[r:r + 1] * o_cmp[gg][:, sl] + sg[r + 1:r + 2] * o_slc[:, sl]
                         + sg[r + 2:r + 3] * o_win[gg][:, sl])
        o_ref[:, qw * gg:qw * (gg + 1)] = jnp.concatenate(heads, axis=0).T.astype(BF16)


def nsa_attention(main3, kvc, gates, wcs_t):
    b = main3.shape[0]
    ng = ATT_GROUPS
    tq = ATT_TILE
    nq = SEQ // tq
    qw = ng * HPG * HEAD_DIM
    kvs_base = Q_W // LANES
    kvw_base = kvs_base + KV_GROUPS

    def kv_spec(base, gg):
        return pl.BlockSpec((None, SEQ, LANES), lambda b_, gp, i: (b_, 0, base + gp * ng + gg))

    return pl.pallas_call(
        _nsa_attn_kernel,
        grid=(b, KV_GROUPS // ng, nq),
        in_specs=[pl.BlockSpec((None, tq, qw), lambda b_, gp, i: (b_, i, gp))]
        + [kv_spec(kvs_base, gg) for gg in range(ng)]
        + [kv_spec(kvw_base, gg) for gg in range(ng)]
        + [
            pl.BlockSpec((None, ng * N_CMP_PAD, LANES), lambda b_, gp, i: (b_, gp, 0)),
            pl.BlockSpec((tq, ng * LANES), lambda b_, gp, i: (b_ * nq + i, gp)),
            pl.BlockSpec((N_SLC, N_CMP_PAD), lambda b_, gp, i: (0, 0)),
        ],
        out_specs=pl.BlockSpec((None, tq, qw), lambda b_, gp, i: (b_, i, gp)),
        out_shape=jax.ShapeDtypeStruct((b, SEQ, Q_W), BF16),
        scratch_shapes=[
            pltpu.VMEM((ng, SEQ, LANES), BF16),
            pltpu.VMEM((ng, SEQ // tq, VT_ROWS, tq), BF16),
            pltpu.VMEM((ng, SEQ // tq, VT_ROWS, tq), BF16),
            pltpu.VMEM((ng, HEAD_DIM, N_CMP_PAD), BF16),
            pltpu.VMEM((ng, SLC_UNROLL, tq, HPG * tq), F32),
            pltpu.VMEM((ng, SLC_UNROLL, tq, HPG * tq), F32),
        ],
        compiler_params=_cparams(("parallel", "parallel", "arbitrary")),
        name="nsa_attention",
    )(main3, *([main3] * (2 * ng)), kvc, gates, wcs_t)


def _proj_norm_res_kernel(x_ref, a_ref, w_ref, g_ref, out_ref):
    out_ref[...] = x_ref[...] + _rms(_dot(a_ref[...], w_ref[...]), g_ref[...])


def proj_norm_res(x, a, w, g, *, tm):
    m, d = x.shape
    k = a.shape[1]
    row = pl.BlockSpec((tm, d), lambda i: (i, 0))
    return pl.pallas_call(
        _proj_norm_res_kernel,
        grid=(m // tm,),
        in_specs=[row, pl.BlockSpec((tm, k), lambda i: (i, 0)), pl.BlockSpec((k, d), lambda i: (0, 0)),
                  pl.BlockSpec((1, d), lambda i: (0, 0))],
        out_specs=row,
        out_shape=jax.ShapeDtypeStruct((m, d), F32),
        compiler_params=_cparams(("parallel",)),
        name="proj_norm_res",
    )(x, a, w, g)


def _sg_out_kernel(x_ref, u_ref, v_ref, lng_ref, lnb_ref, wsp_ref, bt_ref, wo_ref, g1_ref,
                   out_ref, wm_ref, gated_ref, *, tm):
    @pl.when(pl.program_id(0) == 0)
    def _():
        t = lax.broadcasted_iota(jnp.int32, (SG_CHUNK, SG_CHUNK), 0)
        s = lax.broadcasted_iota(jnp.int32, (SG_CHUNK, SG_CHUNK), 1)
        for g in range(SG_GROUPS):
            wm_ref[g] = jnp.where(s <= t, wsp_ref[g], 0.0).astype(BF16)

    v = v_ref[...]
    mu = jnp.mean(v, axis=-1, keepdims=True)
    vc = v - mu
    var = jnp.mean(vc * vc, axis=-1, keepdims=True)
    vn = (vc * lax.rsqrt(var + EPS) * lng_ref[...] + lnb_ref[...]).astype(BF16)
    for c in range(tm // SG_CHUNK):
        r0, r1 = SG_CHUNK * c, SG_CHUNK * (c + 1)
        for g in range(SG_GROUPS):
            c0, c1 = SG_GROUP_W * g, SG_GROUP_W * (g + 1)
            mixed = _dot(wm_ref[g], vn[r0:r1, c0:c1]) + bt_ref[:, g:g + 1]
            gated_ref[r0:r1, c0:c1] = (u_ref[r0:r1, c0:c1] * mixed).astype(BF16)
    m = _dot(gated_ref[...], wo_ref[...])
    out_ref[...] = x_ref[...] + _rms(m, g1_ref[...])


def sg_out(x, z, ln_g, ln_b, w_sp, b_sp_t, w_out, g1, *, tm):
    m, d = x.shape
    e = SG_WIDTH
    row = pl.BlockSpec((tm, d), lambda i: (i, 0))
    return pl.pallas_call(
        functools.partial(_sg_out_kernel, tm=tm),
        grid=(m // tm,),
        in_specs=[
            row,
            pl.BlockSpec((tm, e), lambda i: (i, 0)),
            pl.BlockSpec((tm, e), lambda i: (i, 1)),
            pl.BlockSpec((1, e), lambda i: (0, 0)),
            pl.BlockSpec((1, e), lambda i: (0, 0)),
            pl.BlockSpec((SG_GROUPS, SG_CHUNK, SG_CHUNK), lambda i: (0, 0, 0)),
            pl.BlockSpec((SG_CHUNK, SG_GROUPS), lambda i: (0, 0)),
            pl.BlockSpec((e, d), lambda i: (0, 0)),
            pl.BlockSpec((1, d), lambda i: (0, 0)),
        ],
        out_specs=row,
        out_shape=jax.ShapeDtypeStruct((m, d), F32),
        scratch_shapes=[pltpu.VMEM((SG_GROUPS, SG_CHUNK, SG_CHUNK), BF16), pltpu.VMEM((tm, e), BF16)],
        compiler_params=_cparams(("arbitrary",)),
        name="sg_out",
    )(x, z, z, ln_g, ln_b, w_sp, b_sp_t, w_out, g1)


HALO = BF16_SUBLANES
FFN_CHUNKS = ((0, 2), (2, 6), (6, 10), (10, 14), (14, 18), (18, 20), (20, 22))


def _ffn_kernel(x_ref, xh_ref, g2_ref, wg_ref, wv_ref, cwg_ref, cwv_ref, cbg_ref, cbv_ref,
                wd_ref, g3_ref, out_ref, xn_ref, y_ref, hg0_ref, hg1_ref, hv0_ref, hv1_ref,
                *, tm, tiles_per_seq, chunks):
    i = pl.program_id(0)
    phase_rows = tm // F32_SUBLANES
    d_blocks = D_MODEL // LANES

    xn_ref[HALO:, :] = _rms(x_ref[...], g2_ref[...]).astype(BF16)
    hn = _rms(xh_ref[...], g2_ref[...])
    seq_start = (i % tiles_per_seq) == 0
    xn_ref[:HALO, :] = jnp.where(seq_start, 0.0, hn).astype(BF16)

    hg_refs, hv_refs = (hg0_ref, hg1_ref), (hv0_ref, hv1_ref)

    def phases(h_ref, slot):
        return [h_ref[slot, pl.ds(HALO + p, phase_rows, stride=F32_SUBLANES), :]
                for p in range(1 - CONV_W, F32_SUBLANES)]

    def conv(ph, cw_ref, cb_ref, b, s):
        lo, hi = LANES * b, LANES * (b + 1)
        out = cb_ref[:, lo:hi]
        for k in range(CONV_W):
            out = out + cw_ref[k:k + 1, lo:hi] * ph[s + k]
        return out

    xa = xn_ref[...]

    def up_proj(n):
        b0, b1 = chunks[n]
        c0, c1 = LANES * b0, LANES * b1
        hg = _dot(xa, wg_ref[:, c0:c1])
        hv = _dot(xa, wv_ref[:, c0:c1])
        for b in range(b0, b1):
            hg_refs[n % 2][b - b0] = hg[:, LANES * (b - b0):LANES * (b - b0 + 1)]
            hv_refs[n % 2][b - b0] = hv[:, LANES * (b - b0):LANES * (b - b0 + 1)]

    def conv_act_down(n):
        b0, b1 = chunks[n]
        cols = []
        for b in range(b0, b1):
            pg = phases(hg_refs[n % 2], b - b0)
            pv = phases(hv_refs[n % 2], b - b0)
            rows = []
            for s in range(F32_SUBLANES):
                cg = conv(pg, cwg_ref, cbg_ref, b, s)
                cv = conv(pv, cwv_ref, cbv_ref, b, s)
                rows.append((jax.nn.silu(cg) * cv).astype(BF16))
            cols.append(jnp.concatenate(rows, axis=0))
        act = jnp.concatenate(cols, axis=1)
        return _dot(act, wd_ref[LANES * b0:LANES * b1, :])

    acc = None
    up_proj(0)
    for n in range(len(chunks)):
        if n + 1 < len(chunks):
            up_proj(n + 1)
        part = conv_act_down(n)
        acc = part if acc is None else acc + part

    y = _rms(acc, g3_ref[...])
    for cb in range(d_blocks):
        for s in range(F32_SUBLANES):
            y_ref[cb, pl.ds(s, phase_rows, stride=F32_SUBLANES), :] = (
                y[phase_rows * s:phase_rows * (s + 1), LANES * cb:LANES * (cb + 1)])
    out_ref[...] = x_ref[...] + jnp.concatenate([y_ref[cb] for cb in range(d_blocks)], axis=1)


def conv_ffn(x, g2, w_up, conv_w, conv_b, w_down, g3, *, tm, chunks):
    m, d = x.shape
    f = FFN_HIDDEN
    tiles_per_seq = SEQ // tm
    halo_blocks = tm // HALO
    chunk_blocks = max(b1 - b0 for b0, b1 in chunks)
    once = pl.Buffered(1)
    vec = pl.BlockSpec((1, d), lambda i: (0, 0))
    return pl.pallas_call(
        functools.partial(_ffn_kernel, tm=tm, tiles_per_seq=tiles_per_seq, chunks=chunks),
        grid=(m // tm,),
        in_specs=[
            pl.BlockSpec((tm, d), lambda i: (i, 0)),
            pl.BlockSpec((HALO, d), lambda i: (jnp.maximum(i * halo_blocks - 1, 0), 0)),
            vec,
            pl.BlockSpec((d, f), lambda i: (0, 0), pipeline_mode=once),
            pl.BlockSpec((d, f), lambda i: (0, 1), pipeline_mode=once),
            pl.BlockSpec((CONV_W, f), lambda i: (0, 0)),
            pl.BlockSpec((CONV_W, f), lambda i: (0, 1)),
            pl.BlockSpec((1, f), lambda i: (0, 0)),
            pl.BlockSpec((1, f), lambda i: (0, 1)),
            pl.BlockSpec((f, d), lambda i: (0, 0), pipeline_mode=once),
            vec,
        ],
        out_specs=pl.BlockSpec((tm, d), lambda i: (i, 0)),
        out_shape=jax.ShapeDtypeStruct((m, d), F32),
        scratch_shapes=[
            pltpu.VMEM((HALO + tm, d), BF16),
            pltpu.VMEM((d // LANES, tm, LANES), F32),
        ] + [pltpu.VMEM((chunk_blocks, HALO + tm, LANES), F32)] * 4,
        compiler_params=_cparams(("parallel",)),
        name="conv_ffn",
    )(x, x, g2, w_up, w_up, conv_w, conv_w, conv_b, conv_b, w_down, g3)


def _regroup(w, pair):
    base = Q_W + 2 * pair * KV_W
    kv = w[:, base:base + 2 * KV_W].reshape(D_MODEL, 2, KV_GROUPS, HEAD_DIM)
    return jnp.transpose(kv, (0, 2, 1, 3)).reshape(D_MODEL, KV_PAIR_W)


def _nsa_in_weights(w_in):
    w_gl = w_in[:, Q_W + 6 * KV_W:].reshape(D_MODEL, KV_GROUPS, HPG * N_BRANCH)
    w_gl = jnp.pad(w_gl, ((0, 0), (0, 0), (0, LANES - HPG * N_BRANCH))).reshape(D_MODEL, GATES_W)
    return jnp.concatenate([w_in[:, :Q_W] * QK_SCALE, _regroup(w_in, 1), _regroup(w_in, 2),
                            _regroup(w_in, 0), w_gl], axis=1).astype(BF16)


def _cmp_to_slc_t():
    n_cmp = (SEQ - CMP_LEN) // CMP_STRIDE + 1
    cs = np.arange(n_cmp)[:, None] * CMP_STRIDE
    ss = np.arange(N_SLC)[None, :] * SLC_LEN
    ov = np.clip(np.minimum(cs + CMP_LEN, ss + SLC_LEN) - np.maximum(cs, ss), 0, None)
    w = ov.astype(np.float32) / np.float32(CMP_LEN)
    wt = np.zeros((N_SLC, N_CMP_PAD), dtype=np.float32)
    wt[:, :n_cmp] = w.T
    return jnp.asarray(wt, dtype=BF16)


def _block_diag2(a, b):
    za = jnp.zeros(a.shape[:-1] + (b.shape[-1],), a.dtype)
    zb = jnp.zeros(b.shape[:-1] + (a.shape[-1],), a.dtype)
    return jnp.concatenate([jnp.concatenate([a, za], axis=-1), jnp.concatenate([zb, b], axis=-1)], axis=-2)


def _nsa_layer(x2, batch, g, w_in, cmp_pe, cmp_w1, cmp_w2, w_out):
    w1 = cmp_w1.reshape(2, CMP_LEN, HEAD_DIM, HEAD_DIM)
    w1bd = _block_diag2(w1[0], w1[1])
    wa = w1bd[:CMP_STRIDE].reshape(CMP_STRIDE * LANES, LANES).astype(BF16)
    wb = w1bd[CMP_STRIDE:].reshape(CMP_STRIDE * LANES, LANES).astype(BF16)
    w2bd = _block_diag2(cmp_w2[0], cmp_w2[1]).astype(BF16)
    pe_cat = jnp.concatenate([cmp_pe[0], cmp_pe[1]], axis=1)

    main, cmp, gates = nsa_in_proj(x2, g[0][None], _nsa_in_weights(w_in), tm=512)
    kvc = nsa_compress(cmp.reshape(batch, SEQ, CMP_W), pe_cat, wa, wb, w2bd)
    o = nsa_attention(main.reshape(batch, SEQ, MAIN_W), kvc, gates, _cmp_to_slc_t())
    return proj_norm_res(x2, o.reshape(-1, Q_W), w_out.astype(BF16), g[1][None], tm=512)


def _sg_layer(x2, g, w_in, ln_g, ln_b, w_sp, b_sp, w_out):
    z = norm_matmul_gelu(x2, g[0][None], w_in.astype(BF16), tm=512, chunk=512)
    return sg_out(x2, z, ln_g[None], ln_b[None], w_sp, b_sp.T, w_out.astype(BF16), g[1][None], tm=512)


def kernel(x, norm_gains, nsa_w_in, nsa_cmp_pe, nsa_cmp_w1, nsa_cmp_w2, nsa_w_out, sg_w_in, sg_ln_g,
           sg_ln_b, sg_w_sp, sg_b_sp, sg_w_out, ffn_w_up, ffn_conv_w, ffn_conv_b, ffn_w_down):
    batch, seq, d = x.shape
    assert (seq, d) == (SEQ, D_MODEL)
    x2 = x.reshape(batch * seq, d)
    for i in range(DEPTH):
        g = norm_gains[i]
        slot = i // N_MIXERS
        if i % N_MIXERS == 0:
            x2 = _nsa_layer(x2, batch, g, nsa_w_in[slot], nsa_cmp_pe[slot], nsa_cmp_w1[slot],
                            nsa_cmp_w2[slot], nsa_w_out[slot])
        else:
            x2 = _sg_layer(x2, g, sg_w_in[slot], sg_ln_g[slot], sg_ln_b[slot], sg_w_sp[slot],
                           sg_b_sp[slot], sg_w_out[slot])
        x2 = conv_ffn(x2, g[2][None], ffn_w_up[i].astype(BF16), ffn_conv_w[i], ffn_conv_b[i][None],
                      ffn_w_down[i].astype(BF16), g[3][None], tm=512, chunks=FFN_CHUNKS)
    return x2.reshape(batch, seq, d)
```

```python
import functools

import numpy as np
import jax
import jax.numpy as jnp
from jax import lax
from jax.experimental import pallas as pl
from jax.experimental.pallas import tpu as pltpu

D_MODEL = 1024
SEQ = 2048
DEPTH = 4
N_MIXERS = 2
HEADS = 16
KV_GROUPS = 4
HEAD_DIM = 64
HPG = HEADS // KV_GROUPS
KV_W = KV_GROUPS * HEAD_DIM
N_BRANCH = 3
CMP_LEN = 32
CMP_STRIDE = 16
SLC_LEN = 64
SLC_TOPN = 8
WINDOW = 512
FORCE_SCORE = 1.0e4
NEG_BIG = -1.0e30
SG_WIDTH = 2 * D_MODEL
SG_GROUPS = 16
SG_GROUP_W = SG_WIDTH // SG_GROUPS
SG_CHUNK = 128
FFN_HIDDEN = 2816
CONV_W = 3
EPS = 1e-6

N_SLC = SEQ // SLC_LEN
SLC_SHIFT = SLC_LEN.bit_length() - 1
N_CMP_PAD = SEQ // CMP_STRIDE
Q_W = HEADS * HEAD_DIM

LANES = 128
F32_SUBLANES = 8
BF16_SUBLANES = 16
VMEM_LIMIT = 48 * 1024 * 1024

KV_PAIR_W = KV_GROUPS * LANES
MAIN_W = Q_W + 2 * KV_PAIR_W
CMP_W = KV_PAIR_W
GATES_W = KV_GROUPS * LANES
ATT_TILE = 128
VT_ROWS = HEAD_DIM + BF16_SUBLANES
QK_SCALE = HEAD_DIM ** -0.5 * float(np.log2(np.e))
SLC_UNROLL = 4
ATT_GROUPS = 4

BF16 = jnp.bfloat16
F32 = jnp.float32


def _cparams(sem):
    return pltpu.CompilerParams(dimension_semantics=sem, vmem_limit_bytes=VMEM_LIMIT)


def _rms(x, g):
    return x * lax.rsqrt(jnp.mean(x * x, axis=-1, keepdims=True) + EPS) * g


def _dot(a, b):
    return jnp.dot(a, b, preferred_element_type=F32)


def _norm_matmul_gelu_kernel(x_ref, g_ref, w_ref, o_ref, *, chunk):
    xn = _rms(x_ref[...], g_ref[...]).astype(BF16)
    for c in range(w_ref.shape[1] // chunk):
        o_ref[:, chunk * c:chunk * (c + 1)] = jax.nn.gelu(_dot(xn, w_ref[:, chunk * c:chunk * (c + 1)]))


def norm_matmul_gelu(x, g, w, *, tm, chunk):
    m, k = x.shape
    n = w.shape[1]
    return pl.pallas_call(
        functools.partial(_norm_matmul_gelu_kernel, chunk=chunk),
        grid=(m // tm,),
        in_specs=[
            pl.BlockSpec((tm, k), lambda i: (i, 0)),
            pl.BlockSpec((1, k), lambda i: (0, 0)),
            pl.BlockSpec((k, n), lambda i: (0, 0), pipeline_mode=pl.Buffered(1)),
        ],
        out_specs=pl.BlockSpec((tm, n), lambda i: (i, 0)),
        out_shape=jax.ShapeDtypeStruct((m, n), F32),
        compiler_params=_cparams(("parallel",)),
        name="norm_matmul_gelu",
    )(x, g, w)


def _nsa_in_kernel(x_ref, g_ref, w_ref, main_ref, cmp_ref, gate_ref):
    xn = _rms(x_ref[...], g_ref[...]).astype(BF16)
    main_ref[...] = _dot(xn, w_ref[:, :MAIN_W]).astype(BF16)
    cmp_ref[...] = _dot(xn, w_ref[:, MAIN_W:MAIN_W + CMP_W])
    gate_ref[...] = _dot(xn, w_ref[:, MAIN_W + CMP_W:])


def nsa_in_proj(x, g, w_all, *, tm):
    m, d = x.shape
    n = w_all.shape[1]
    return pl.pallas_call(
        _nsa_in_kernel,
        grid=(m // tm,),
        in_specs=[
            pl.BlockSpec((tm, d), lambda i: (i, 0)),
            pl.BlockSpec((1, d), lambda i: (0, 0)),
            pl.BlockSpec((d, n), lambda i: (0, 0)),
        ],
        out_specs=[
            pl.BlockSpec((tm, MAIN_W), lambda i: (i, 0)),
            pl.BlockSpec((tm, CMP_W), lambda i: (i, 0)),
            pl.BlockSpec((tm, GATES_W), lambda i: (i, 0)),
        ],
        out_shape=[
            jax.ShapeDtypeStruct((m, MAIN_W), BF16),
            jax.ShapeDtypeStruct((m, CMP_W), F32),
            jax.ShapeDtypeStruct((m, GATES_W), F32),
        ],
        compiler_params=_cparams(("parallel",)),
        name="nsa_in_proj",
    )(x, g, w_all)


def _compress_kernel(x0_ref, x1_ref, x2_ref, x3_ref, pe_ref, wa_ref, wb_ref, w2_ref, kv_ref,
                     ca_ref, cb_ref):
    nck = N_CMP_PAD
    for l in range(CMP_STRIDE):
        for g, x_ref in enumerate((x0_ref, x1_ref, x2_ref, x3_ref)):
            xg = x_ref[pl.ds(l, nck, stride=CMP_STRIDE), :]
            ca_ref[nck * g:nck * (g + 1), LANES * l:LANES * (l + 1)] = (
                xg + pe_ref[l:l + 1, :]).astype(BF16)
            cb_ref[nck * g:nck * (g + 1), LANES * l:LANES * (l + 1)] = (
                xg + pe_ref[CMP_STRIDE + l:CMP_STRIDE + l + 1, :]).astype(BF16)
    a = _dot(ca_ref[...], wa_ref[...])
    bm = _dot(cb_ref[...], wb_ref[...])
    for g in range(KV_GROUPS):
        ag = a[nck * g:nck * (g + 1)]
        bg = bm[nck * g:nck * (g + 1)]
        pre = ag + pltpu.roll(bg, nck - 1, 0)
        h = jax.nn.gelu(pre).astype(BF16)
        kv_ref[nck * g:nck * (g + 1), :] = _dot(h, w2_ref[...]).astype(BF16)


def nsa_compress(cmp3, pe_cat, wa, wb, w2bd):
    b = cmp3.shape[0]
    rows = KV_GROUPS * N_CMP_PAD
    feat = CMP_STRIDE * LANES
    x_specs = [pl.BlockSpec((None, SEQ, LANES), functools.partial(lambda i, g: (i, 0, g), g=g))
               for g in range(KV_GROUPS)]
    return pl.pallas_call(
        _compress_kernel,
        grid=(b,),
        in_specs=x_specs + [
            pl.BlockSpec((CMP_LEN, LANES), lambda i: (0, 0)),
            pl.BlockSpec((feat, LANES), lambda i: (0, 0)),
            pl.BlockSpec((feat, LANES), lambda i: (0, 0)),
            pl.BlockSpec((LANES, LANES), lambda i: (0, 0)),
        ],
        out_specs=pl.BlockSpec((None, rows, LANES), lambda i: (i, 0, 0)),
        out_shape=jax.ShapeDtypeStruct((b, rows, LANES), BF16),
        scratch_shapes=[pltpu.VMEM((rows, feat), BF16), pltpu.VMEM((rows, feat), BF16)],
        compiler_params=_cparams(("parallel",)),
        name="nsa_compress",
    )(cmp3, cmp3, cmp3, cmp3, pe_cat, wa, wb, w2bd)


def _mask_rows(s, mask, fill, tq):
    return jnp.concatenate(
        [jnp.where(mask, s[:, tq * h:tq * (h + 1)], fill) for h in range(HPG)], axis=1)


def _scores(tiles, q4t, tq):
    out = []
    for lhs, mask in tiles:
        s = _dot(lhs, q4t)
        out.append(s if mask is None else _mask_rows(s, mask, NEG_BIG, tq))
    return out


def _softmax_pv(carry, ss, vts):
    m_new = None if carry is None else carry[0]
    for s in ss:
        mx = jnp.max(s, axis=0, keepdims=True)
        m_new = mx if m_new is None else jnp.maximum(m_new, mx)
    acc = None if carry is None else jnp.exp2(carry[0] - m_new) * carry[1]
    for s, vt in zip(ss, vts):
        pv = _dot(vt, jnp.exp2(s - m_new).astype(BF16))
        acc = pv if acc is None else acc + pv
    return m_new, acc


def _normalized(acc):
    return acc[:HEAD_DIM] * (1.0 / acc[HEAD_DIM:HEAD_DIM + 1])


def _nsa_attn_kernel(*refs):
    ng = ATT_GROUPS
    q_ref = refs[0]
    kvs_refs = refs[1:1 + ng]
    kvw_refs = refs[1 + ng:1 + 2 * ng]
    kvc_ref, gl_ref, wt_ref, o_ref, kse_ref, vst_ref, vwt_ref, vct_ref, s0_ref, s1_ref = refs[1 + 2 * ng:]
    tq = tk = ATT_TILE
    unroll = SLC_UNROLL
    qw = HPG * HEAD_DIM
    lanes4 = HPG * tq
    win_tiles = WINDOW // tk
    qi = pl.program_id(2)
    t0 = qi * tq
    groups = range(ng)

    @pl.when(qi == 0)
    def _():
        lane = lax.broadcasted_iota(jnp.int32, (tk, LANES), 1)
        krow = lax.broadcasted_iota(jnp.int32, (tk, LANES), 0)
        ones_pad = (lax.broadcasted_iota(jnp.int32, (VT_ROWS - HEAD_DIM, tk), 0) == 0).astype(F32)

        def vt_tile(kv):
            return jnp.concatenate([kv.T[HEAD_DIM:, :], ones_pad], axis=0).astype(BF16)

        for gg in groups:
            for c in range(SEQ // tk):
                kv = kvs_refs[gg][tk * c:tk * (c + 1), :].astype(F32)
                vst_ref[gg, c] = vt_tile(kv)
                vwt_ref[gg, c] = vt_tile(kvw_refs[gg][tk * c:tk * (c + 1), :].astype(F32))
                onehot = jnp.where(lane - HEAD_DIM == ((tk * c + krow) >> SLC_SHIFT), 1.0, 0.0)
                kse_ref[gg, tk * c:tk * (c + 1), :] = jnp.where(lane < HEAD_DIM, kv, onehot).astype(BF16)
            vct_ref[gg] = kvc_ref[N_CMP_PAD * gg:N_CMP_PAD * (gg + 1), :].astype(F32).T[HEAD_DIM:, :].astype(BF16)

    kr = lax.broadcasted_iota(jnp.int32, (tk, tq), 0)
    tc = lax.broadcasted_iota(jnp.int32, (tk, tq), 1)
    diag = kr <= tc
    zpad = jnp.zeros((HEAD_DIM, tq), F32)

    qts = [q_ref[:, qw * gg:qw * (gg + 1)].astype(F32).T for gg in groups]

    def stack_q(qt, extra):
        return jnp.concatenate(
            [jnp.concatenate([qt[HEAD_DIM * h:HEAD_DIM * (h + 1)], extra], axis=0) for h in range(HPG)],
            axis=1).astype(BF16)

    q4t = [stack_q(qts[gg], zpad) for gg in groups]

    win_s, win_vt, cmp_s = [], [], []
    for gg in groups:
        tiles, vts = [], []
        for u in range(win_tiles + 1):
            kt = qi - win_tiles + u
            exists = kt >= 0
            ktc = jnp.maximum(kt, 0)
            k0 = pl.multiple_of(ktc * tk, tk)
            if u == 0:
                mask = jnp.logical_and(kr > tc, exists)
            elif u < win_tiles:
                mask = jnp.logical_and(kr >= 0, exists)
            else:
                mask = diag
            tiles.append((kvw_refs[gg][pl.ds(k0, tk), :], mask))
            vts.append(vwt_ref[gg, ktc])
        win_s.append(_scores(tiles, q4t[gg], tq))
        win_vt.append(vts)
        cmp_s.append(_dot(kvc_ref[N_CMP_PAD * gg:N_CMP_PAD * (gg + 1), :], q4t[gg]))

    cmask = (kr * CMP_STRIDE + (CMP_LEN - 1)) <= (t0 + tc)
    o_cmp, imp4 = [], []
    for gg in groups:
        s = _mask_rows(cmp_s[gg], cmask, NEG_BIG, tq)
        m = jnp.max(s, axis=0, keepdims=True)
        e = _mask_rows(jnp.exp2(s - m), cmask, 0.0, tq)
        den = jnp.sum(e, axis=0, keepdims=True)
        p = (e * (1.0 / jnp.maximum(den, 1e-30))).astype(BF16)
        o_cmp.append(_dot(vct_ref[gg], p))
        imp4.append(_dot(wt_ref[...], p))

    o_win = []
    for gg in groups:
        o_win.append(_normalized(_softmax_pv(None, win_s[gg], win_vt[gg])[1]))

    j = lax.broadcasted_iota(jnp.int32, (N_SLC, tq), 0)
    blk = (t0 + lax.broadcasted_iota(jnp.int32, (N_SLC, tq), 1)) >> SLC_SHIFT
    valid = j <= blk
    forced = (j == 0) | (j == blk) | (j == blk - 1)
    q4t_sel = []
    for gg in groups:
        imp = imp4[gg][:, 0:tq]
        for h in range(1, HPG):
            imp = imp + imp4[gg][:, tq * h:tq * (h + 1)]
        score = jnp.where(valid, imp + jnp.where(forced, FORCE_SCORE, 0.0), -FORCE_SCORE)
        rank = jnp.zeros(imp.shape, F32)
        for i in range(N_SLC):
            ri = score[i:i + 1, :]
            beats = jnp.where(ri > score, 1.0, jnp.where(ri == score, jnp.where(j > i, 1.0, 0.0), 0.0))
            rank = rank + beats
        sel_bias = jnp.where(valid, jnp.where(rank < float(SLC_TOPN), 0.0, NEG_BIG), NEG_BIG)
        q4t_sel.append(stack_q(qts[gg], jnp.concatenate([sel_bias, jnp.zeros((HEAD_DIM - N_SLC, tq), F32)], axis=0)))

    s_refs = (s0_ref, s1_ref)
    n_main = qi // unroll
    tail_base = n_main * unroll
    for gg in groups:
        for u in range(unroll):
            k0 = pl.multiple_of((tail_base + u) * tk, tk)
            s = _dot(kse_ref[gg, pl.ds(k0, tk), :], q4t_sel[gg])
            s0_ref[gg, u] = _mask_rows(s, (k0 + kr) <= (t0 + tc), NEG_BIG, tq)

    def process(parity):
        def run(args):
            i, carries = args
            base = jnp.where(i == 0, tail_base, (i - 1) * unroll)
            return tuple(
                _softmax_pv(carries[gg], [s_refs[parity][gg, u] for u in range(unroll)],
                            [vst_ref[gg, base + u] for u in range(unroll)]) for gg in groups)
        return run

    def step(parity):
        def run(args):
            i, carries = args
            for gg in groups:
                for u in range(unroll):
                    k0 = pl.multiple_of((i * unroll + u) * tk, tk)
                    s_refs[1 - parity][gg, u] = _dot(kse_ref[gg, pl.ds(k0, tk), :], q4t_sel[gg])
            return process(parity)(args)
        return run

    init = (jnp.full((1, lanes4), NEG_BIG, F32), jnp.zeros((VT_ROWS, lanes4), F32))
    carries = lax.fori_loop(
        0, n_main, lambda i, c: lax.cond(i % 2 == 0, step(0), step(1), (i, c)), (init,) * ng)
    carries = lax.cond(n_main % 2 == 0, process(0), process(1), (n_main, carries))

    for gg in groups:
        o_slc = _normalized(carries[gg][1])
        sg = jax.nn.sigmoid(gl_ref[:, LANES * gg:LANES * (gg + 1)].T[:BF16_SUBLANES, :])
        heads = []
        for h in range(HPG):
            sl = slice(tq * h, tq * (h + 1))
            r = N_BRANCH * h
            heads.append(sg[r:r + 1] * o_cmp[gg][:, sl] + sg[r + 1:r + 2] * o_slc[:, sl]
                         + sg[r + 2:r + 3] * o_win[gg][:, sl])
        o_ref[:, qw * gg:qw * (gg + 1)] = jnp.concatenate(heads, axis=0).T.astype(BF16)


def nsa_attention(main3, kvc, gates, wcs_t):
    b = main3.shape[0]
    ng = ATT_GROUPS
    tq = ATT_TILE
    nq = SEQ // tq
    qw = ng * HPG * HEAD_DIM
    kvs_base = Q_W // LANES
    kvw_base = kvs_base + KV_GROUPS

    def kv_spec(base, gg):
        return pl.BlockSpec((None, SEQ, LANES), lambda b_, gp, i: (b_, 0, base + gp * ng + gg))

    return pl.pallas_call(
        _nsa_attn_kernel,
        grid=(b, KV_GROUPS // ng, nq),
        in_specs=[pl.BlockSpec((None, tq, qw), lambda b_, gp, i: (b_, i, gp))]
        + [kv_spec(kvs_base, gg) for gg in range(ng)]
        + [kv_spec(kvw_base, gg) for gg in range(ng)]
        + [
            pl.BlockSpec((None, ng * N_CMP_PAD, LANES), lambda b_, gp, i: (b_, gp, 0)),
            pl.BlockSpec((tq, ng * LANES), lambda b_, gp, i: (b_ * nq + i, gp)),
            pl.BlockSpec((N_SLC, N_CMP_PAD), lambda b_, gp, i: (0, 0)),
        ],
        out_specs=pl.BlockSpec((None, tq, qw), lambda b_, gp, i: (b_, i, gp)),
        out_shape=jax.ShapeDtypeStruct((b, SEQ, Q_W), BF16),
        scratch_shapes=[
            pltpu.VMEM((ng, SEQ, LANES), BF16),
            pltpu.VMEM((ng, SEQ // tq, VT_ROWS, tq), BF16),
            pltpu.VMEM((ng, SEQ // tq, VT_ROWS, tq), BF16),
            pltpu.VMEM((ng, HEAD_DIM, N_CMP_PAD), BF16),
            pltpu.VMEM((ng, SLC_UNROLL, tq, HPG * tq), F32),
            pltpu.VMEM((ng, SLC_UNROLL, tq, HPG * tq), F32),
        ],
        compiler_params=_cparams(("parallel", "parallel", "arbitrary")),
        name="nsa_attention",
    )(main3, *([main3] * (2 * ng)), kvc, gates, wcs_t)


def _proj_norm_res_kernel(x_ref, a_ref, w_ref, g_ref, out_ref):
    out_ref[...] = x_ref[...] + _rms(_dot(a_ref[...], w_ref[...]), g_ref[...])


def proj_norm_res(x, a, w, g, *, tm):
    m, d = x.shape
    k = a.shape[1]
    row = pl.BlockSpec((tm, d), lambda i: (i, 0))
    return pl.pallas_call(
        _proj_norm_res_kernel,
        grid=(m // tm,),
        in_specs=[row, pl.BlockSpec((tm, k), lambda i: (i, 0)), pl.BlockSpec((k, d), lambda i: (0, 0)),
                  pl.BlockSpec((1, d), lambda i: (0, 0))],
        out_specs=row,
        out_shape=jax.ShapeDtypeStruct((m, d), F32),
        compiler_params=_cparams(("parallel",)),
        name="proj_norm_res",
    )(x, a, w, g)


def _sg_out_kernel(x_ref, u_ref, v_ref, lng_ref, lnb_ref, wsp_ref, bt_ref, wo_ref, g1_ref,
                   out_ref, wm_ref, gated_ref, *, tm):
    @pl.when(pl.program_id(0) == 0)
    def _():
        t = lax.broadcasted_iota(jnp.int32, (SG_CHUNK, SG_CHUNK), 0)
        s = lax.broadcasted_iota(jnp.int32, (SG_CHUNK, SG_CHUNK), 1)
        for g in range(SG_GROUPS):
            wm_ref[g] = jnp.where(s <= t, wsp_ref[g], 0.0).astype(BF16)

    v = v_ref[...]
    mu = jnp.mean(v, axis=-1, keepdims=True)
    vc = v - mu
    var = jnp.mean(vc * vc, axis=-1, keepdims=True)
    vn = (vc * lax.rsqrt(var + EPS) * lng_ref[...] + lnb_ref[...]).astype(BF16)
    quarters = 4
    per_q = SG_GROUPS // quarters
    chunks = range(tm // SG_CHUNK)

    def mix(q):
        return [[_dot(wm_ref[g], vn[SG_CHUNK * c:SG_CHUNK * (c + 1), SG_GROUP_W * g:SG_GROUP_W * (g + 1)])
                 for g in range(per_q * q, per_q * (q + 1))] for c in chunks]

    def gate(q, mixed):
        rows = []
        for c in chunks:
            r0, r1 = SG_CHUNK * c, SG_CHUNK * (c + 1)
            cols = []
            for k, g in enumerate(range(per_q * q, per_q * (q + 1))):
                c0, c1 = SG_GROUP_W * g, SG_GROUP_W * (g + 1)
                cols.append((u_ref[r0:r1, c0:c1] * (mixed[c][k] + bt_ref[:, g:g + 1])).astype(BF16))
            rows.append(jnp.concatenate(cols, axis=1))
        return jnp.concatenate(rows, axis=0)

    m = None
    cur = mix(0)
    for q in range(quarters):
        nxt = mix(q + 1) if q + 1 < quarters else None
        k0, k1 = per_q * SG_GROUP_W * q, per_q * SG_GROUP_W * (q + 1)
        part = _dot(gate(q, cur), wo_ref[k0:k1, :])
        m = part if m is None else m + part
        cur = nxt
    out_ref[...] = x_ref[...] + _rms(m, g1_ref[...])


def sg_out(x, z, ln_g, ln_b, w_sp, b_sp_t, w_out, g1, *, tm):
    m, d = x.shape
    e = SG_WIDTH
    row = pl.BlockSpec((tm, d), lambda i: (i, 0))
    return pl.pallas_call(
        functools.partial(_sg_out_kernel, tm=tm),
        grid=(m // tm,),
        in_specs=[
            row,
            pl.BlockSpec((tm, e), lambda i: (i, 0)),
            pl.BlockSpec((tm, e), lambda i: (i, 1)),
            pl.BlockSpec((1, e), lambda i: (0, 0)),
            pl.BlockSpec((1, e), lambda i: (0, 0)),
            pl.BlockSpec((SG_GROUPS, SG_CHUNK, SG_CHUNK), lambda i: (0, 0, 0)),
            pl.BlockSpec((SG_CHUNK, SG_GROUPS), lambda i: (0, 0)),
            pl.BlockSpec((e, d), lambda i: (0, 0)),
            pl.BlockSpec((1, d), lambda i: (0, 0)),
        ],
        out_specs=row,
        out_shape=jax.ShapeDtypeStruct((m, d), F32),
        scratch_shapes=[pltpu.VMEM((SG_GROUPS, SG_CHUNK, SG_CHUNK), BF16), pltpu.VMEM((tm, e), BF16)],
        compiler_params=_cparams(("arbitrary",)),
        name="sg_out",
    )(x, z, z, ln_g, ln_b, w_sp, b_sp_t, w_out, g1)


HALO = BF16_SUBLANES
FFN_CHUNKS = ((0, 2), (2, 6), (6, 10), (10, 14), (14, 18), (18, 20), (20, 22))


def _ffn_kernel(x_ref, xh_ref, g2_ref, wg_ref, wv_ref, cwg_ref, cwv_ref, cbg_ref, cbv_ref,
                wd_ref, g3_ref, out_ref, xn_ref, y_ref, hg0_ref, hg1_ref, hv0_ref, hv1_ref,
                *, tm, tiles_per_seq, chunks):
    i = pl.program_id(0)
    phase_rows = tm // F32_SUBLANES
    d_blocks = D_MODEL // LANES

    xn_ref[HALO:, :] = _rms(x_ref[...], g2_ref[...]).astype(BF16)
    hn = _rms(xh_ref[...], g2_ref[...])
    seq_start = (i % tiles_per_seq) == 0
    xn_ref[:HALO, :] = jnp.where(seq_start, 0.0, hn).astype(BF16)

    hg_refs, hv_refs = (hg0_ref, hg1_ref), (hv0_ref, hv1_ref)

    def phases(h_ref, slot):
        return [h_ref[slot, pl.ds(HALO + p, phase_rows, stride=F32_SUBLANES), :]
                for p in range(1 - CONV_W, F32_SUBLANES)]

    def conv(ph, cw_ref, cb_ref, b, s):
        lo, hi = LANES * b, LANES * (b + 1)
        out = cb_ref[:, lo:hi]
        for k in range(CONV_W):
            out = out + cw_ref[k:k + 1, lo:hi] * ph[s + k]
        return out

    xa = xn_ref[...]

    def up_proj(n):
        b0, b1 = chunks[n]
        c0, c1 = LANES * b0, LANES * b1
        hg = _dot(xa, wg_ref[:, c0:c1])
        hv = _dot(xa, wv_ref[:, c0:c1])
        for b in range(b0, b1):
            hg_refs[n % 2][b - b0] = hg[:, LANES * (b - b0):LANES * (b - b0 + 1)]
            hv_refs[n % 2][b - b0] = hv[:, LANES * (b - b0):LANES * (b - b0 + 1)]

    def conv_act_down(n):
        b0, b1 = chunks[n]
        cols = []
        for b in range(b0, b1):
            pg = phases(hg_refs[n % 2], b - b0)
            pv = phases(hv_refs[n % 2], b - b0)
            rows = []
            for s in range(F32_SUBLANES):
                cg = conv(pg, cwg_ref, cbg_ref, b, s)
                cv = conv(pv, cwv_ref, cbv_ref, b, s)
                rows.append((jax.nn.silu(cg) * cv).astype(BF16))
            cols.append(jnp.concatenate(rows, axis=0))
        act = jnp.concatenate(cols, axis=1)
        return _dot(act, wd_ref[LANES * b0:LANES * b1, :])

    acc = None
    up_proj(0)
    for n in range(len(chunks)):
        if n + 1 < len(chunks):
            up_proj(n + 1)
        part = conv_act_down(n)
        acc = part if acc is None else acc + part

    y = _rms(acc, g3_ref[...])
    for cb in range(d_blocks):
        for s in range(F32_SUBLANES):
            y_ref[cb, pl.ds(s, phase_rows, stride=F32_SUBLANES), :] = (
                y[phase_rows * s:phase_rows * (s + 1), LANES * cb:LANES * (cb + 1)])
    out_ref[...] = x_ref[...] + jnp.concatenate([y_ref[cb] for cb in range(d_blocks)], axis=1)


def conv_ffn(x, g2, w_up, conv_w, conv_b, w_down, g3, *, tm, chunks):
    m, d = x.shape
    f = FFN_HIDDEN
    tiles_per_seq = SEQ // tm
    halo_blocks = tm // HALO
    chunk_blocks = max(b1 - b0 for b0, b1 in chunks)
    once = pl.Buffered(1)
    vec = pl.BlockSpec((1, d), lambda i: (0, 0))
    return pl.pallas_call(
        functools.partial(_ffn_kernel, tm=tm, tiles_per_seq=tiles_per_seq, chunks=chunks),
        grid=(m // tm,),
        in_specs=[
            pl.BlockSpec((tm, d), lambda i: (i, 0)),
            pl.BlockSpec((HALO, d), lambda i: (jnp.maximum(i * halo_blocks - 1, 0), 0)),
            vec,
            pl.BlockSpec((d, f), lambda i: (0, 0), pipeline_mode=once),
            pl.BlockSpec((d, f), lambda i: (0, 1), pipeline_mode=once),
            pl.BlockSpec((CONV_W, f), lambda i: (0, 0)),
            pl.BlockSpec((CONV_W, f), lambda i: (0, 1)),
            pl.BlockSpec((1, f), lambda i: (0, 0)),
            pl.BlockSpec((1, f), lambda i: (0, 1)),
            pl.BlockSpec((f, d), lambda i: (0, 0), pipeline_mode=once),
            vec,
        ],
        out_specs=pl.BlockSpec((tm, d), lambda i: (i, 0)),
        out_shape=jax.ShapeDtypeStruct((m, d), F32),
        scratch_shapes=[
            pltpu.VMEM((HALO + tm, d), BF16),
            pltpu.VMEM((d // LANES, tm, LANES), F32),
        ] + [pltpu.VMEM((chunk_blocks, HALO + tm, LANES), F32)] * 4,
        compiler_params=_cparams(("parallel",)),
        name="conv_ffn",
    )(x, x, g2, w_up, w_up, conv_w, conv_w, conv_b, conv_b, w_down, g3)


def _regroup(w, pair):
    base = Q_W + 2 * pair * KV_W
    kv = w[:, base:base + 2 * KV_W].reshape(D_MODEL, 2, KV_GROUPS, HEAD_DIM)
    return jnp.transpose(kv, (0, 2, 1, 3)).reshape(D_MODEL, KV_PAIR_W)


def _nsa_in_weights(w_in):
    w_gl = w_in[:, Q_W + 6 * KV_W:].reshape(D_MODEL, KV_GROUPS, HPG * N_BRANCH)
    w_gl = jnp.pad(w_gl, ((0, 0), (0, 0), (0, LANES - HPG * N_BRANCH))).reshape(D_MODEL, GATES_W)
    return jnp.concatenate([w_in[:, :Q_W] * QK_SCALE, _regroup(w_in, 1), _regroup(w_in, 2),
                            _regroup(w_in, 0), w_gl], axis=1).astype(BF16)


def _cmp_to_slc_t():
    n_cmp = (SEQ - CMP_LEN) // CMP_STRIDE + 1
    cs = np.arange(n_cmp)[:, None] * CMP_STRIDE
    ss = np.arange(N_SLC)[None, :] * SLC_LEN
    ov = np.clip(np.minimum(cs + CMP_LEN, ss + SLC_LEN) - np.maximum(cs, ss), 0, None)
    w = ov.astype(np.float32) / np.float32(CMP_LEN)
    wt = np.zeros((N_SLC, N_CMP_PAD), dtype=np.float32)
    wt[:, :n_cmp] = w.T
    return jnp.asarray(wt, dtype=BF16)


def _block_diag2(a, b):
    za = jnp.zeros(a.shape[:-1] + (b.shape[-1],), a.dtype)
    zb = jnp.zeros(b.shape[:-1] + (a.shape[-1],), a.dtype)
    return jnp.concatenate([jnp.concatenate([a, za], axis=-1), jnp.concatenate([zb, b], axis=-1)], axis=-2)


def _nsa_layer(x2, batch, g, w_in, cmp_pe, cmp_w1, cmp_w2, w_out):
    w1 = cmp_w1.reshape(2, CMP_LEN, HEAD_DIM, HEAD_DIM)
    w1bd = _block_diag2(w1[0], w1[1])
    wa = w1bd[:CMP_STRIDE].reshape(CMP_STRIDE * LANES, LANES).astype(BF16)
    wb = w1bd[CMP_STRIDE:].reshape(CMP_STRIDE * LANES, LANES).astype(BF16)
    w2bd = _block_diag2(cmp_w2[0], cmp_w2[1]).astype(BF16)
    pe_cat = jnp.concatenate([cmp_pe[0], cmp_pe[1]], axis=1)

    main, cmp, gates = nsa_in_proj(x2, g[0][None], _nsa_in_weights(w_in), tm=512)
    kvc = nsa_compress(cmp.reshape(batch, SEQ, CMP_W), pe_cat, wa, wb, w2bd)
    o = nsa_attention(main.reshape(batch, SEQ, MAIN_W), kvc, gates, _cmp_to_slc_t())
    return proj_norm_res(x2, o.reshape(-1, Q_W), w_out.astype(BF16), g[1][None], tm=512)


def _sg_layer(x2, g, w_in, ln_g, ln_b, w_sp, b_sp, w_out):
    z = norm_matmul_gelu(x2, g[0][None], w_in.astype(BF16), tm=512, chunk=512)
    return sg_out(x2, z, ln_g[None], ln_b[None], w_sp, b_sp.T, w_out.astype(BF16), g[1][None], tm=512)


def kernel(x, norm_gains, nsa_w_in, nsa_cmp_pe, nsa_cmp_w1, nsa_cmp_w2, nsa_w_out, sg_w_in, sg_ln_g,
           sg_ln_b, sg_w_sp, sg_b_sp, sg_w_out, ffn_w_up, ffn_conv_w, ffn_conv_b, ffn_w_down):
    batch, seq, d = x.shape
    assert (seq, d) == (SEQ, D_MODEL)
    x2 = x.reshape(batch * seq, d)
    for i in range(DEPTH):
        g = norm_gains[i]
        slot = i // N_MIXERS
        if i % N_MIXERS == 0:
            x2 = _nsa_layer(x2, batch, g, nsa_w_in[slot], nsa_cmp_pe[slot], nsa_cmp_w1[slot],
                            nsa_cmp_w2[slot], nsa_w_out[slot])
        else:
            x2 = _sg_layer(x2, g, sg_w_in[slot], sg_ln_g[slot], sg_ln_b[slot], sg_w_sp[slot],
                           sg_b_sp[slot], sg_w_out[slot])
        x2 = conv_ffn(x2, g[2][None], ffn_w_up[i].astype(BF16), ffn_conv_w[i], ffn_conv_b[i][None],
                      ffn_w_down[i].astype(BF16), g[3][None], tm=512, chunks=FFN_CHUNKS)
    return x2.reshape(batch, seq, d)
```
